```python
import jax, jax.numpy as jnp
from jax import lax
import numpy as np

D_MODEL = 2048
BATCH = 2
SEQ = 8192
DEPTH = 2

HEAD_DIM = 128
MIX_WIDTH = D_MODEL
A_HEADS = MIX_WIDTH // 2 // HEAD_DIM
B_HEADS = MIX_WIDTH // 4 // HEAD_DIM
POOL_GROUPS = 4
POOL_GROUP_DIM = MIX_WIDTH // 4 // POOL_GROUPS
POOL_WINDOWS = (2, 4, 8, 16)
A_WIDTH = A_HEADS * HEAD_DIM
B_WIDTH = B_HEADS * HEAD_DIM
C_WIDTH = POOL_GROUPS * POOL_GROUP_DIM
IN_COLS = 3 * A_WIDTH + 3 * B_WIDTH + C_WIDTH
MOBA_BLOCK = 256
MOBA_TOPK = 3
MOBA_Q_CHUNK = 32
DIL_PAIRS = ((128, 1), (512, 4), (2048, 16))
MEM_LEN = 256
MEM_HEADS = 4
MEM_WIDTH = MEM_HEADS * HEAD_DIM
N_GROUPS = 4
EXPERTS_PER_GROUP = 8
EXPERT_FF = D_MODEL // 4
EXPERT_TOPK = 2
RMS_EPS = 1e-6

kernel_name = "hybrid_moba_dilated_pool_hmoe_trunk"


def rmsnorm(x, g):
    xf = x.astype(jnp.float32)
    y = xf * lax.rsqrt(jnp.mean(xf * xf, axis=-1, keepdims=True) + RMS_EPS)
    return (y * g.astype(jnp.float32)).astype(x.dtype)


def alibi_slopes():
    n = A_HEADS + B_HEADS
    s = 2.0 ** (-8.0 * np.arange(1, n + 1) / n)
    is_b = (np.arange(n) % 3) == 2
    return jnp.asarray(s[~is_b], jnp.float32), jnp.asarray(s[is_b], jnp.float32)


def to_heads(t, n_heads):
    b, s, _ = t.shape
    return t.reshape(b, s, n_heads, HEAD_DIM).transpose(0, 2, 1, 3)


def from_heads(t):
    b, h, s, d = t.shape
    return t.transpose(0, 2, 1, 3).reshape(b, s, h * d)


def moba_attention(q, k, v, slopes):
    b, h, s, dh = q.shape
    nblk = -(-s // MOBA_BLOCK)
    sp = nblk * MOBA_BLOCK
    pad = ((0, 0), (0, 0), (0, sp - s), (0, 0))
    q, k, v = jnp.pad(q, pad), jnp.pad(k, pad), jnp.pad(v, pad)
    kblocks = k.reshape(b, h, nblk, MOBA_BLOCK, dh)
    vblocks = v.reshape(b, h, nblk, MOBA_BLOCK, dh)
    kmean = jnp.mean(kblocks.astype(jnp.float32), axis=3)
    topk = min(MOBA_TOPK, nblk)
    scale = HEAD_DIM ** -0.5
    n_chunks = sp // MOBA_Q_CHUNK
    qc = q.reshape(b, h, n_chunks, MOBA_Q_CHUNK, dh).transpose(2, 0, 1, 3, 4)
    bi = jnp.arange(b)[:, None, None, None]
    hi = jnp.arange(h)[None, :, None, None]
    blk_ids = jnp.arange(nblk)
    offs = jnp.arange(MOBA_BLOCK)

    def chunk_fn(args):
        ci, qch = args
        start = ci * MOBA_Q_CHUNK
        own = start // MOBA_BLOCK
        tpos = start + jnp.arange(MOBA_Q_CHUNK)
        gate = jnp.einsum('bhqd,bhnd->bhqn', qch.astype(jnp.float32), kmean)
        gate = jnp.where((blk_ids < own)[None, None, None, :], gate, -jnp.inf)
        _, idx = lax.top_k(gate, topk)
        sel_valid = idx < own
        ksel = kblocks[bi, hi, idx]
        vsel = vblocks[bi, hi, idx]
        kown = lax.dynamic_index_in_dim(kblocks, own, axis=2, keepdims=False)
        vown = lax.dynamic_index_in_dim(vblocks, own, axis=2, keepdims=False)
        s_sel = jnp.einsum('bhqd,bhqjkd->bhqjk', qch, ksel).astype(jnp.float32) * scale
        dist_sel = (tpos[None, None, :, None, None] - (idx[..., None] * MOBA_BLOCK + offs)).astype(jnp.float32)
        s_sel = jnp.where(sel_valid[..., None],
                          s_sel - slopes[None, :, None, None, None] * dist_sel, -jnp.inf)
        s_own = jnp.einsum('bhqd,bhkd->bhqk', qch, kown).astype(jnp.float32) * scale
        dist_own = tpos[:, None] - (own * MOBA_BLOCK + offs)[None, :]
        s_own = jnp.where((dist_own >= 0)[None, None],
                          s_own - slopes[None, :, None, None] * dist_own.astype(jnp.float32)[None, None],
                          -jnp.inf)
        s_all = jnp.concatenate([s_sel.reshape(b, h, MOBA_Q_CHUNK, topk * MOBA_BLOCK), s_own], axis=-1)
        p = jax.nn.softmax(s_all, axis=-1).astype(v.dtype)
        p_sel = p[..., :topk * MOBA_BLOCK].reshape(b, h, MOBA_Q_CHUNK, topk, MOBA_BLOCK)
        p_own = p[..., topk * MOBA_BLOCK:]
        return (jnp.einsum('bhqjk,bhqjkd->bhqd', p_sel, vsel)
                + jnp.einsum('bhqk,bhkd->bhqd', p_own, vown))

    o = lax.map(chunk_fn, (jnp.arange(n_chunks), qc))
    o = o.transpose(1, 2, 0, 3, 4).reshape(b, h, sp, dh)
    return o[:, :, :s]


def dilated_branch(q, k, v, slopes, n_off, dil):
    b, h, sp, dh = q.shape
    L = sp // dil
    nb = L // n_off

    def split(t):
        return t.reshape(b, h, L, dil, dh).transpose(0, 1, 3, 2, 4).reshape(b, h, dil, nb, n_off, dh)

    def with_prev(t):
        prev = jnp.pad(t[:, :, :, :-1], ((0, 0), (0, 0), (0, 0), (1, 0), (0, 0), (0, 0)))
        return jnp.concatenate([prev, t], axis=4)

    qb = split(q)
    kk, vv = with_prev(split(k)), with_prev(split(v))
    s = jnp.einsum('bhrnqd,bhrnkd->bhrnqk', qb, kk).astype(jnp.float32) * (HEAD_DIM ** -0.5)
    qi = jnp.arange(n_off)[:, None]
    ki = jnp.arange(2 * n_off)[None, :]
    j = qi + n_off - ki
    blk = jnp.arange(nb)[:, None, None]
    valid = (j >= 0) & (j <= n_off) & ((blk > 0) | (ki >= n_off))
    penalty = slopes[None, :, None, None, None, None] * (j * dil).astype(jnp.float32)
    s = jnp.where(valid[None, None, None], s - penalty, -jnp.inf)
    lse = jax.nn.logsumexp(s, axis=-1)
    p = jnp.exp(s - lse[..., None]).astype(v.dtype)
    o = jnp.einsum('bhrnqk,bhrnkd->bhrnqd', p, vv)
    o = o.reshape(b, h, dil, L, dh).transpose(0, 1, 3, 2, 4).reshape(b, h, sp, dh)
    lse = lse.reshape(b, h, dil, L).transpose(0, 1, 3, 2).reshape(b, h, sp)
    return o, lse


def dilated_attention(q, k, v, slopes):
    b, h, s, dh = q.shape
    unit = max(d for _, d in DIL_PAIRS) * max(w // d for w, d in DIL_PAIRS)
    sp = -(-s // unit) * unit
    pad = ((0, 0), (0, 0), (0, sp - s), (0, 0))
    q, k, v = jnp.pad(q, pad), jnp.pad(k, pad), jnp.pad(v, pad)
    outs, lses = [], []
    for window, dil in DIL_PAIRS:
        o, lse = dilated_branch(q, k, v, slopes, window // dil, dil)
        outs.append(o)
        lses.append(lse)
    w = jax.nn.softmax(jnp.stack(lses), axis=0)
    o = jnp.einsum('gbhs,gbhsd->bhsd', w, jnp.stack(outs).astype(jnp.float32))
    return o[:, :, :s].astype(q.dtype)


def pool_mixer(u, w_pool, scale):
    b, s, _ = u.shape
    ug = u.reshape(b, s, POOL_GROUPS, POOL_GROUP_DIM).astype(jnp.float32)
    cs = jnp.concatenate([jnp.zeros((b, 1, POOL_GROUPS, POOL_GROUP_DIM), jnp.float32),
                          jnp.cumsum(ug, axis=1)], axis=1)
    t = jnp.arange(s)[:, None]
    win = jnp.asarray(POOL_WINDOWS)[None, :]
    lo = jnp.maximum(t + 1 - win, 0)
    cnt = (t + 1 - lo).astype(jnp.float32)
    cs_lo = cs[:, lo, jnp.arange(POOL_GROUPS)[None, :]]
    mean = (cs[:, 1:] - cs_lo) / cnt[None, :, :, None]
    pooled = (mean - ug).astype(u.dtype)
    y = jnp.einsum('bsgc,gcd->bsgd', pooled, w_pool) * scale.reshape(POOL_GROUPS, POOL_GROUP_DIM)
    return y.reshape(b, s, C_WIDTH)


def memory_cross_attention(h, mem, g_memkv, w_mq, w_mk, w_mv, w_mo):
    b, s, _ = h.shape
    mn = rmsnorm(mem, g_memkv)
    q = (h @ w_mq).reshape(b, s, MEM_HEADS, HEAD_DIM)
    k = (mn @ w_mk).reshape(b, -1, MEM_HEADS, HEAD_DIM)
    v = (mn @ w_mv).reshape(b, -1, MEM_HEADS, HEAD_DIM)
    sc = jnp.einsum('bshd,bmhd->bhsm', q, k).astype(jnp.float32) * (HEAD_DIM ** -0.5)
    p = jax.nn.softmax(sc, axis=-1).astype(v.dtype)
    o = jnp.einsum('bhsm,bmhd->bshd', p, v).reshape(b, s, MEM_WIDTH)
    return o @ w_mo


def hier_moe(h, w_group, b_group, w_router, b_router, w1, w3, w2):
    b, s, d = h.shape
    xt = h.reshape(b * s, d)
    glog = (xt @ w_group).astype(jnp.float32) + b_group.astype(jnp.float32)
    gprob = jax.nn.softmax(glog, axis=-1)
    gsel = jnp.argmax(glog, axis=-1)
    gw = jnp.take_along_axis(gprob, gsel[:, None], axis=1)[:, 0]
    elog_all = jnp.einsum('td,gde->tge', xt, w_router).astype(jnp.float32) + b_router.astype(jnp.float32)
    elog = jnp.take_along_axis(elog_all, gsel[:, None, None], axis=1)[:, 0]
    top_v, top_i = lax.top_k(elog, EXPERT_TOPK)
    ew = jax.nn.softmax(top_v, axis=-1) * gw[:, None]
    e_dense = jnp.sum(jax.nn.one_hot(top_i, EXPERTS_PER_GROUP, dtype=jnp.float32) * ew[..., None], axis=1)
    combine = (jax.nn.one_hot(gsel, N_GROUPS, dtype=jnp.float32)[:, :, None] * e_dense[:, None, :]).astype(h.dtype)
    y = jnp.zeros((b * s, d), h.dtype)
    for g in range(N_GROUPS):
        a = jnp.einsum('td,edf->tef', xt, w1[g])
        c = jnp.einsum('td,edf->tef', xt, w3[g])
        hid = jax.nn.silu(a) * c * combine[:, g, :, None]
        y = y + jnp.einsum('tef,efd->td', hid, w2[g])
    return y.reshape(b, s, d)


def setup_inputs(seed: int = 0) -> dict:
    key = jax.random.key(seed)
    ks = jax.random.split(key, 24)
    f32 = jnp.float32
    L, D, G, E, F = DEPTH, D_MODEL, N_GROUPS, EXPERTS_PER_GROUP, EXPERT_FF

    def nrm(k, shape, fan_in):
        return jax.random.normal(k, shape, f32) * (fan_in ** -0.5)

    def gain(k, shape):
        return 1.0 + 0.02 * jax.random.normal(k, shape, f32)

    return {
        "x": jax.random.normal(ks[0], (BATCH, SEQ, D), f32),
        "mem": jax.random.normal(ks[1], (BATCH, MEM_LEN, D), f32),
        "g_mix": gain(ks[2], (L, D)),
        "w_in": nrm(ks[3], (L, D, IN_COLS), D),
        "w_out": nrm(ks[4], (L, MIX_WIDTH, D), MIX_WIDTH),
        "pool_w": nrm(ks[5], (L, POOL_GROUPS, POOL_GROUP_DIM, POOL_GROUP_DIM), POOL_GROUP_DIM),
        "pool_scale": gain(ks[6], (L, C_WIDTH)),
        "g_mem": gain(ks[7], (L, D)),
        "g_memkv": gain(ks[8], (L, D)),
        "w_mq": nrm(ks[9], (L, D, MEM_WIDTH), D),
        "w_mk": nrm(ks[10], (L, D, MEM_WIDTH), D),
        "w_mv": nrm(ks[11], (L, D, MEM_WIDTH), D),
        "w_mo": nrm(ks[12], (L, MEM_WIDTH, D), MEM_WIDTH),
        "g_ffn": gain(ks[13], (L, D)),
        "w_group": nrm(ks[14], (L, D, G), D),
        "b_group": 0.01 * jax.random.normal(ks[15], (L, G), f32),
        "w_router": nrm(ks[16], (L, G, D, E), D),
        "b_router": 0.01 * jax.random.normal(ks[17], (L, G, E), f32),
        "w1": nrm(ks[18], (L, G, E, D, F), D),
        "w3": nrm(ks[19], (L, G, E, D, F), D),
        "w2": nrm(ks[20], (L, G, E, F, D), F),
        "g_final": gain(ks[21], (D,)),
    }


def reference(x, mem, g_mix, w_in, w_out, pool_w, pool_scale, g_mem, g_memkv, w_mq, w_mk, w_mv,
              w_mo, g_ffn, w_group, b_group, w_router, b_router, w1, w3, w2, g_final):
    slopes_a, slopes_b = alibi_slopes()
    splits = [A_WIDTH, 2 * A_WIDTH, 3 * A_WIDTH, 3 * A_WIDTH + B_WIDTH,
              3 * A_WIDTH + 2 * B_WIDTH, 3 * A_WIDTH + 3 * B_WIDTH]
    for l in range(DEPTH):
        h = rmsnorm(x, g_mix[l])
        proj = h @ w_in[l]
        qa, ka, va, qb, kb, vb, u = jnp.split(proj, splits, axis=-1)
        oa = from_heads(moba_attention(to_heads(qa, A_HEADS), to_heads(ka, A_HEADS),
                                       to_heads(va, A_HEADS), slopes_a))
        ob = from_heads(dilated_attention(to_heads(qb, B_HEADS), to_heads(kb, B_HEADS),
                                          to_heads(vb, B_HEADS), slopes_b))
        oc = pool_mixer(u, pool_w[l], pool_scale[l])
        x = x + jnp.concatenate([oa, ob, oc], axis=-1) @ w_out[l]
        x = x + memory_cross_attention(rmsnorm(x, g_mem[l]), mem, g_memkv[l], w_mq[l], w_mk[l],
                                       w_mv[l], w_mo[l])
        x = x + hier_moe(rmsnorm(x, g_ffn[l]), w_group[l], b_group[l], w_router[l], b_router[l],
                         w1[l], w3[l], w2[l])
    return rmsnorm(x, g_final)
```

```python
import functools

import numpy as np
import jax
import jax.numpy as jnp
from jax import lax
from jax.experimental import pallas as pl
from jax.experimental.pallas import tpu as pltpu

D_MODEL = 2048
HEAD_DIM = 128
A_HEADS = 8
B_HEADS = 4
POOL_GROUPS = 4
POOL_WINDOWS = (2, 4, 8, 16)
A_WIDTH = A_HEADS * HEAD_DIM
B_WIDTH = B_HEADS * HEAD_DIM
C_WIDTH = POOL_GROUPS * HEAD_DIM
MOBA_BLOCK = 256
MOBA_TOPK = 3
DIL_PAIRS = ((128, 1), (512, 4), (2048, 16))
DIL_OFFSETS = 128
DIL_UNIT = 2048
MEM_HEADS = 4
MEM_WIDTH = MEM_HEADS * HEAD_DIM
N_GROUPS = 4
EXPERTS_PER_GROUP = 8
N_EXPERTS = N_GROUPS * EXPERTS_PER_GROUP
EXPERT_FF = D_MODEL // 4
RMS_EPS = 1e-6
SCALE = HEAD_DIM ** -0.5

LANES = 128
MOBA_SLOTS = 32
EXPERT_TILE = 256
NEG = -(2.0 ** 100)
VMEM_LIMIT = 56 * 1024 * 1024

BF = jnp.bfloat16
F32 = jnp.float32
NT_DIMS = (((1,), (1,)), ((), ()))


def _params(*sem):
    return pltpu.CompilerParams(dimension_semantics=sem, vmem_limit_bytes=VMEM_LIMIT)


def _rms(x, g):
    return x * lax.rsqrt(jnp.mean(x * x, axis=-1, keepdims=True) + RMS_EPS) * g


def _split3(x):
    hi = x.astype(BF).astype(F32)
    r1 = x - hi
    mid = r1.astype(BF).astype(F32)
    lo = (r1 - mid).astype(BF).astype(F32)
    return hi, mid, lo


def _norm_matmul_kernel(x_ref, g_ref, w_ref, cs_ref, o_ref, hn_ref):
    @pl.when(pl.program_id(1) == 0)
    def _():
        hn_ref[...] = _rms(x_ref[...], g_ref[...]).astype(BF)

    acc = jnp.dot(hn_ref[...], w_ref[...], preferred_element_type=F32)
    o_ref[...] = (acc * cs_ref[...]).astype(o_ref.dtype)


def norm_matmul(x, g, w, colscale, out_dtype, tm, tn):
    m, d = x.shape
    n = w.shape[1]
    return pl.pallas_call(
        _norm_matmul_kernel,
        grid=(m // tm, n // tn),
        in_specs=[
            pl.BlockSpec((tm, d), lambda i, j: (i, 0)),
            pl.BlockSpec((1, d), lambda i, j: (0, 0)),
            pl.BlockSpec((d, tn), lambda i, j: (0, j)),
            pl.BlockSpec((1, tn), lambda i, j: (0, j)),
        ],
        out_specs=pl.BlockSpec((tm, tn), lambda i, j: (i, j)),
        out_shape=jax.ShapeDtypeStruct((m, n), out_dtype),
        scratch_shapes=[pltpu.VMEM((tm, d), BF)],
        compiler_params=_params("parallel", "arbitrary"),
        name="norm_matmul",
    )(x, g, w, colscale)


def _moba_kernel(slopes_ref, q_ref, k_ref, v_ref, kaug_ref, o_ref, kfull_ref, kmean_ref, *, bq, nblk):
    h = pl.program_id(1)
    i = pl.program_id(2)
    nq = bq // MOBA_BLOCK

    @pl.when(i == 0)
    def _():
        kfull_ref[:, :HEAD_DIM] = k_ref[...]
        kfull_ref[:, HEAD_DIM:] = kaug_ref[...]
        km = jnp.mean(k_ref[...].astype(F32).reshape(nblk, MOBA_BLOCK, HEAD_DIM), axis=1)
        if nblk < MOBA_SLOTS:
            km = jnp.concatenate([km, jnp.zeros((MOBA_SLOTS - nblk, HEAD_DIM), F32)], axis=0)
        pad = jnp.zeros((LANES - 3 * MOBA_SLOTS, HEAD_DIM), F32)
        kmean_ref[...] = jnp.concatenate([km, km, km, pad], axis=0).astype(BF)

    q = q_ref[...]
    gate = lax.dot_general(q, kmean_ref[...], NT_DIMS, preferred_element_type=F32)
    lane = lax.broadcasted_iota(jnp.int32, (bq, LANES), 1)
    row = lax.broadcasted_iota(jnp.int32, (bq, LANES), 0)
    qblk = i * nq + row // MOBA_BLOCK
    kblk = lane & (MOBA_SLOTS - 1)
    is_feat = lane < 3 * MOBA_SLOTS
    past = jnp.logical_and(kblk < qblk, is_feat)
    lanef = lane.astype(F32)
    kblkf = kblk.astype(F32)
    g = jnp.where(past, gate, -jnp.inf)
    sel = jnp.zeros((bq, LANES), jnp.bool_)
    for _ in range(MOBA_TOPK):
        mx = jnp.max(g, axis=1, keepdims=True)
        first = jnp.min(jnp.where(g == mx, lanef, 1e9), axis=1, keepdims=True)
        pick = kblkf == first
        sel = jnp.logical_or(sel, pick)
        g = jnp.where(pick, -jnp.inf, g)
    sel = jnp.logical_and(sel, past)

    slope = slopes_ref[h]
    blkdist = (qblk - kblk).astype(F32) * float(MOBA_BLOCK)
    own = jnp.logical_and(kblk == qblk, is_feat)
    bias = jnp.where(sel, -slope * blkdist, jnp.where(own, 0.0, NEG))
    b_hi, b_mid, b_lo = _split3(bias)
    s_hi, s_mid, s_lo = _split3(jnp.full((bq, LANES), slope, F32))
    qaug = jnp.where(lane < MOBA_SLOTS, b_hi,
           jnp.where(lane < 2 * MOBA_SLOTS, b_mid,
           jnp.where(lane < 3 * MOBA_SLOTS, b_lo,
           jnp.where(lane == 3 * MOBA_SLOTS, s_hi,
           jnp.where(lane == 3 * MOBA_SLOTS + 1, s_mid,
           jnp.where(lane == 3 * MOBA_SLOTS + 2, s_lo, 0.0))))))
    q_full = jnp.concatenate([q, qaug.astype(BF)], axis=1)

    def tile(j, carry, diagonal):
        m, l, acc = carry
        start = pl.multiple_of(j * MOBA_BLOCK, MOBA_BLOCK)
        kj = kfull_ref[pl.ds(start, MOBA_BLOCK), :]
        vj = v_ref[pl.ds(start, MOBA_BLOCK), :]
        s = lax.dot_general(q_full, kj, NT_DIMS, preferred_element_type=F32)
        if diagonal:
            qpos = i * bq + lax.broadcasted_iota(jnp.int32, (bq, MOBA_BLOCK), 0)
            kpos = j * MOBA_BLOCK + lax.broadcasted_iota(jnp.int32, (bq, MOBA_BLOCK), 1)
            s = jnp.where(qpos >= kpos, s, NEG)
        m_new = jnp.maximum(m, jnp.max(s, axis=1, keepdims=True))
        alpha = jnp.exp(m - m_new)
        p = jnp.exp(s - m_new)
        l = alpha * l + jnp.sum(p, axis=1, keepdims=True)
        acc = alpha * acc + jnp.dot(p.astype(BF), vj, preferred_element_type=F32)
        return m_new, l, acc

    carry = (jnp.full((bq, 1), -jnp.inf, F32), jnp.zeros((bq, 1), F32), jnp.zeros((bq, HEAD_DIM), F32))
    carry = lax.fori_loop(0, i * nq, lambda j, c: tile(j, c, False), carry)
    for d in range(nq):
        carry = tile(i * nq + d, carry, True)
    _, l, acc = carry
    o_ref[...] = (acc / l).astype(o_ref.dtype)


def moba_attention(pa, kaug, slopes, bq):
    b, s, _ = pa.shape
    nblk = s // MOBA_BLOCK
    assert s % bq == 0 and bq % MOBA_BLOCK == 0 and nblk <= MOBA_SLOTS
    return pl.pallas_call(
        functools.partial(_moba_kernel, bq=bq, nblk=nblk),
        grid=(b, A_HEADS, s // bq),
        in_specs=[
            pl.BlockSpec(memory_space=pltpu.SMEM),
            pl.BlockSpec((None, bq, HEAD_DIM), lambda bi, h, i: (bi, i, h)),
            pl.BlockSpec((None, s, HEAD_DIM), lambda bi, h, i: (bi, 0, A_HEADS + h)),
            pl.BlockSpec((None, s, HEAD_DIM), lambda bi, h, i: (bi, 0, 2 * A_HEADS + h)),
            pl.BlockSpec((s, LANES), lambda bi, h, i: (0, 0)),
        ],
        out_specs=pl.BlockSpec((None, bq, HEAD_DIM), lambda bi, h, i: (bi, i, h)),
        out_shape=jax.ShapeDtypeStruct((b, s, A_WIDTH), BF),
        scratch_shapes=[pltpu.VMEM((s, 2 * HEAD_DIM), BF), pltpu.VMEM((LANES, HEAD_DIM), BF)],
        compiler_params=_params("parallel", "parallel", "arbitrary"),
        name="moba_attention",
    )(slopes, pa, pa, pa, kaug)


def moba_key_features(s):
    t = np.arange(s)
    f = np.arange(LANES)
    onehot = ((f[None, :] % MOBA_SLOTS) == (t[:, None] // MOBA_BLOCK)) & (f[None, :] < 3 * MOBA_SLOTS)
    ramp = ((f[None, :] >= 3 * MOBA_SLOTS) & (f[None, :] < 3 * MOBA_SLOTS + 3)) * (t[:, None] % MOBA_BLOCK)
    return jnp.asarray(onehot.astype(np.float32) + ramp.astype(np.float32), BF)


def _dil_kernel(slopes_ref, q_ref, kc_ref, kp_ref, vc_ref, vp_ref, o_ref,
                kext_ref, vext_ref, acc_ref, m_ref, l_ref):
    h = pl.program_id(1)
    u = pl.program_id(2)
    slope = slopes_ref[h]
    n = DIL_OFFSETS
    kext_ref[:DIL_UNIT, :] = kp_ref[...]
    kext_ref[DIL_UNIT:, :] = kc_ref[...]
    vext_ref[:DIL_UNIT, :] = vp_ref[...]
    vext_ref[DIL_UNIT:, :] = vc_ref[...]

    qi = lax.broadcasted_iota(jnp.int32, (n, 2 * n), 0)
    ki = lax.broadcasted_iota(jnp.int32, (n, 2 * n), 1)
    steps = qi + n - ki
    valid = jnp.logical_and(steps >= 0, steps <= n)
    stepsf = steps.astype(F32)
    in_cur = ki >= n

    for gi, (window, dil) in enumerate(DIL_PAIRS):
        nbk = DIL_UNIT // (dil * n)
        pen = jnp.where(valid, -slope * float(dil) * stepsf, NEG)

        def body(idx, carry, dil=dil, nbk=nbk, pen=pen, gi=gi):
            r = idx // nbk
            blk = idx % nbk
            start = blk * (n * dil) + r
            q = q_ref[pl.ds(start, n, stride=dil), :].astype(BF)
            kstart = DIL_UNIT + start - n * dil
            kk = kext_ref[pl.ds(kstart, 2 * n, stride=dil), :].astype(BF)
            vv = vext_ref[pl.ds(kstart, 2 * n, stride=dil), :].astype(BF)
            s = lax.dot_general(q, kk, NT_DIMS, preferred_element_type=F32) + pen
            prev_ok = jnp.logical_or(u > 0, blk > 0)
            s = jnp.where(jnp.logical_or(in_cur, prev_ok), s, NEG)
            m = jnp.max(s, axis=1, keepdims=True)
            p = jnp.exp(s - m)
            l = jnp.sum(p, axis=1, keepdims=True)
            o = jnp.dot(p.astype(BF), vv, preferred_element_type=F32)
            acc_ref.at[gi][pl.ds(start, n, stride=dil), :] = o
            m_ref.at[gi][pl.ds(start, n, stride=dil), :] = jnp.broadcast_to(m, (n, HEAD_DIM))
            l_ref.at[gi][pl.ds(start, n, stride=dil), :] = jnp.broadcast_to(l, (n, HEAD_DIM))
            return carry

        lax.fori_loop(0, dil * nbk, body, 0)

    m_all = jnp.maximum(jnp.maximum(m_ref[0], m_ref[1]), m_ref[2])
    num = jnp.zeros((DIL_UNIT, HEAD_DIM), F32)
    den = jnp.zeros((DIL_UNIT, HEAD_DIM), F32)
    for gi in range(len(DIL_PAIRS)):
        w = jnp.exp(m_ref[gi] - m_all)
        num = num + w * acc_ref[gi]
        den = den + w * l_ref[gi]
    o_ref[...] = (num / den).astype(o_ref.dtype)


def dilated_attention(pb, slopes):
    b, s, _ = pb.shape
    assert s % DIL_UNIT == 0
    cur = lambda off: (lambda bi, h, u: (bi, u, off + h))
    prev = lambda off: (lambda bi, h, u: (bi, jnp.maximum(u - 1, 0), off + h))
    blk = (None, DIL_UNIT, HEAD_DIM)
    return pl.pallas_call(
        _dil_kernel,
        grid=(b, B_HEADS, s // DIL_UNIT),
        in_specs=[
            pl.BlockSpec(memory_space=pltpu.SMEM),
            pl.BlockSpec(blk, cur(0)),
            pl.BlockSpec(blk, cur(B_HEADS)),
            pl.BlockSpec(blk, prev(B_HEADS)),
            pl.BlockSpec(blk, cur(2 * B_HEADS)),
            pl.BlockSpec(blk, prev(2 * B_HEADS)),
        ],
        out_specs=pl.BlockSpec(blk, cur(0)),
        out_shape=jax.ShapeDtypeStruct((b, s, B_WIDTH), BF),
        scratch_shapes=[
            pltpu.VMEM((2 * DIL_UNIT, HEAD_DIM), F32),
            pltpu.VMEM((2 * DIL_UNIT, HEAD_DIM), F32),
            pltpu.VMEM((3, DIL_UNIT, HEAD_DIM), F32),
            pltpu.VMEM((3, DIL_UNIT, HEAD_DIM), F32),
            pltpu.VMEM((3, DIL_UNIT, HEAD_DIM), F32),
        ],
        compiler_params=_params("parallel", "parallel", "arbitrary"),
        name="dilated_attention",
    )(slopes, pb, pb, pb, pb, pb)


POOL_HALO = 16


def _pool_kernel(u_ref, halo_ref, w_ref, sc_ref, o_ref, ext_ref, *, tm):
    i = pl.program_id(1)
    halo = halo_ref[...]
    ext_ref[:POOL_HALO, :] = jnp.where(i > 0, halo, jnp.zeros_like(halo))
    ext_ref[POOL_HALO:, :] = u_ref[...]
    t = i * tm + lax.broadcasted_iota(jnp.int32, (tm, 1), 0)
    for gi, win in enumerate(POOL_WINDOWS):
        cols = slice(gi * HEAD_DIM, (gi + 1) * HEAD_DIM)
        tot = ext_ref[POOL_HALO:, cols]
        for back in range(1, win):
            tot = tot + ext_ref[pl.ds(POOL_HALO - back, tm), cols]
        cnt = jnp.minimum(t + 1, win).astype(F32)
        pooled = (tot / cnt - ext_ref[POOL_HALO:, cols]).astype(BF)
        y = jnp.dot(pooled, w_ref[gi], preferred_element_type=F32)
        o_ref[:, cols] = (y * sc_ref[:, cols]).astype(o_ref.dtype)


def pool_mixer(pb, pool_w, pool_scale, tm):
    b, s, _ = pb.shape
    ucol = 3 * B_WIDTH // C_WIDTH
    return pl.pallas_call(
        functools.partial(_pool_kernel, tm=tm),
        grid=(b, s // tm),
        in_specs=[
            pl.BlockSpec((None, tm, C_WIDTH), lambda bi, i: (bi, i, ucol)),
            pl.BlockSpec((None, POOL_HALO, C_WIDTH),
                         lambda bi, i: (bi, jnp.maximum(i * (tm // POOL_HALO) - 1, 0), ucol)),
            pl.BlockSpec((POOL_GROUPS, HEAD_DIM, HEAD_DIM), lambda bi, i: (0, 0, 0)),
            pl.BlockSpec((1, C_WIDTH), lambda bi, i: (0, 0)),
        ],
        out_specs=pl.BlockSpec((None, tm, C_WIDTH), lambda bi, i: (bi, i, 0)),
        out_shape=jax.ShapeDtypeStruct((b, s, C_WIDTH), BF),
        scratch_shapes=[pltpu.VMEM((tm + POOL_HALO, C_WIDTH), F32)],
        compiler_params=_params("parallel", "parallel"),
        name="pool_mixer",
    )(pb, pb, pool_w, pool_scale)


def _out_proj_kernel(x_ref, oa_ref, ob_ref, oc_ref, w_ref, o_ref):
    acc = jnp.dot(oa_ref[...], w_ref[:A_WIDTH, :], preferred_element_type=F32)
    acc += jnp.dot(ob_ref[...], w_ref[A_WIDTH:A_WIDTH + B_WIDTH, :], preferred_element_type=F32)
    acc += jnp.dot(oc_ref[...], w_ref[A_WIDTH + B_WIDTH:, :], preferred_element_type=F32)
    o_ref[...] = x_ref[...] + acc


def out_proj(x, oa, ob, oc, w, tm):
    t, d = x.shape
    return pl.pallas_call(
        _out_proj_kernel,
        grid=(t // tm,),
        in_specs=[
            pl.BlockSpec((tm, d), lambda i: (i, 0)),
            pl.BlockSpec((tm, A_WIDTH), lambda i: (i, 0)),
            pl.BlockSpec((tm, B_WIDTH), lambda i: (i, 0)),
            pl.BlockSpec((tm, C_WIDTH), lambda i: (i, 0)),
            pl.BlockSpec(w.shape, lambda i: (0, 0)),
        ],
        out_specs=pl.BlockSpec((tm, d), lambda i: (i, 0)),
        out_shape=jax.ShapeDtypeStruct((t, d), F32),
        compiler_params=_params("parallel"),
        name="out_proj",
    )(x, oa, ob, oc, w)


def _cross_kernel(x_ref, g_ref, wq_ref, kv_ref, wo_ref, o_ref):
    x = x_ref[...]
    hn = _rms(x, g_ref[...]).astype(BF)
    q = (jnp.dot(hn, wq_ref[...], preferred_element_type=F32) * SCALE).astype(BF)
    outs = []
    for hd in range(MEM_HEADS):
        cols = slice(hd * HEAD_DIM, (hd + 1) * HEAD_DIM)
        k = kv_ref[:, cols]
        v = kv_ref[:, MEM_WIDTH + hd * HEAD_DIM:MEM_WIDTH + (hd + 1) * HEAD_DIM]
        s = lax.dot_general(q[:, cols], k, NT_DIMS, preferred_element_type=F32)
        m = jnp.max(s, axis=1, keepdims=True)
        p = jnp.exp(s - m)
        l = jnp.sum(p, axis=1, keepdims=True)
        outs.append((jnp.dot(p.astype(BF), v, preferred_element_type=F32) / l).astype(BF))
    o = jnp.concatenate(outs, axis=1)
    o_ref[...] = x + jnp.dot(o, wo_ref[...], preferred_element_type=F32)


def cross_attention(x3, g, wq, kv, wo, tm):
    b, s, d = x3.shape
    mem_len = kv.shape[1]
    return pl.pallas_call(
        _cross_kernel,
        grid=(b, s // tm),
        in_specs=[
            pl.BlockSpec((None, tm, d), lambda bi, i: (bi, i, 0)),
            pl.BlockSpec((1, d), lambda bi, i: (0, 0)),
            pl.BlockSpec(wq.shape, lambda bi, i: (0, 0)),
            pl.BlockSpec((None, mem_len, 2 * MEM_WIDTH), lambda bi, i: (bi, 0, 0)),
            pl.BlockSpec(wo.shape, lambda bi, i: (0, 0)),
        ],
        out_specs=pl.BlockSpec((None, tm, d), lambda bi, i: (bi, i, 0)),
        out_shape=jax.ShapeDtypeStruct((b, s, d), F32),
        compiler_params=_params("parallel", "parallel"),
        name="cross_attention",
    )(x3, g, wq, kv, wo)


def _router_kernel(x_ref, g_ref, whi_ref, wlo_ref, b_ref, hn_ref, meta_ref, cnt_ref, base_ref, *, tm):
    @pl.when(pl.program_id(0) == 0)
    def _():
        base_ref[...] = jnp.zeros_like(base_ref)

    hn = _rms(x_ref[...], g_ref[...])
    hn_ref[...] = hn
    hi = hn.astype(BF)
    lo = (hn - hi.astype(F32)).astype(BF)
    logits = (jnp.dot(hi, whi_ref[...], preferred_element_type=F32)
              + jnp.dot(lo, whi_ref[...], preferred_element_type=F32)
              + jnp.dot(hi, wlo_ref[...], preferred_element_type=F32)) + b_ref[...]
    lane = lax.broadcasted_iota(jnp.int32, (tm, LANES), 1)
    lanef = lane.astype(F32)

    def first_argmax(vals):
        mx = jnp.max(vals, axis=1, keepdims=True)
        idx = jnp.min(jnp.where(vals == mx, lanef, 1e9), axis=1, keepdims=True)
        return mx, idx

    is_group = lane < N_GROUPS
    gmax, gsel = first_argmax(jnp.where(is_group, logits, -jnp.inf))
    gsum = jnp.sum(jnp.where(is_group, jnp.exp(logits - gmax), 0.0), axis=1, keepdims=True)
    gw = 1.0 / gsum
    lane_group = ((lane - N_GROUPS) // EXPERTS_PER_GROUP).astype(F32)
    in_group = jnp.logical_and(jnp.logical_and(lane >= N_GROUPS, lane < N_GROUPS + N_EXPERTS),
                               lane_group == gsel)
    el = jnp.where(in_group, logits, -jnp.inf)
    v1, i1 = first_argmax(el)
    v2, i2 = first_argmax(jnp.where(lanef == i1, -jnp.inf, el))
    t = jnp.exp(v2 - v1)
    w1 = gw / (1.0 + t)
    w2 = gw * t / (1.0 + t)
    e1 = i1 - N_GROUPS
    e2 = i2 - N_GROUPS
    oh1 = lanef == e1
    oh2 = lanef == e2
    cnt = jnp.where(jnp.logical_or(oh1, oh2), 1.0, 0.0)
    r = lax.broadcasted_iota(jnp.int32, (tm, tm), 0)
    c = lax.broadcasted_iota(jnp.int32, (tm, tm), 1)
    lower = jnp.where(r > c, 1.0, 0.0).astype(BF)
    before = jnp.dot(lower, cnt.astype(BF), preferred_element_type=F32) + base_ref[...]
    rank1 = jnp.sum(jnp.where(oh1, before, 0.0), axis=1, keepdims=True)
    rank2 = jnp.sum(jnp.where(oh2, before, 0.0), axis=1, keepdims=True)
    base_ref[...] += jnp.sum(cnt, axis=0, keepdims=True)
    cnt_ref[...] = base_ref[...]
    meta_ref[...] = jnp.where(lane == 0, e1, jnp.where(lane == 1, e2, jnp.where(lane == 2, rank1,
                    jnp.where(lane == 3, rank2, jnp.where(lane == 4, w1, jnp.where(lane == 5, w2, 0.0))))))


def router(x, g, whi, wlo, bias, tm):
    t, d = x.shape
    return pl.pallas_call(
        functools.partial(_router_kernel, tm=tm),
        grid=(t // tm,),
        in_specs=[
            pl.BlockSpec((tm, d), lambda i: (i, 0)),
            pl.BlockSpec((1, d), lambda i: (0, 0)),
            pl.BlockSpec((d, LANES), lambda i: (0, 0)),
            pl.BlockSpec((d, LANES), lambda i: (0, 0)),
            pl.BlockSpec((1, LANES), lambda i: (0, 0)),
        ],
        out_specs=[
            pl.BlockSpec((tm, d), lambda i: (i, 0)),
            pl.BlockSpec((tm, LANES), lambda i: (i, 0)),
            pl.BlockSpec((1, LANES), lambda i: (0, 0)),
        ],
        out_shape=[
            jax.ShapeDtypeStruct((t, d), F32),
            jax.ShapeDtypeStruct((t, LANES), F32),
            jax.ShapeDtypeStruct((1, LANES), F32),
        ],
        scratch_shapes=[pltpu.VMEM((1, LANES), F32)],
        compiler_params=_params("arbitrary"),
        name="moe_router",
    )(x, g, whi, wlo, bias)


def _positions_kernel(meta_ref, offs_ref, o_ref, *, tm):
    meta = meta_ref[...]
    lane = lax.broadcasted_iota(jnp.int32, (tm, LANES), 1)
    lanef = lane.astype(F32)
    offs = offs_ref[...]
    pos1 = jnp.sum(jnp.where(lanef == meta[:, 0:1], offs, 0.0), axis=1, keepdims=True) + meta[:, 2:3]
    pos2 = jnp.sum(jnp.where(lanef == meta[:, 1:2], offs, 0.0), axis=1, keepdims=True) + meta[:, 3:4]
    o_ref[...] = jnp.where(lane == 0, pos1, jnp.where(lane == 1, pos2, 0.0))


def positions(meta, offs, tm):
    t = meta.shape[0]
    return pl.pallas_call(
        functools.partial(_positions_kernel, tm=tm),
        grid=(t // tm,),
        in_specs=[pl.BlockSpec((tm, LANES), lambda i: (i, 0)), pl.BlockSpec((1, LANES), lambda i: (0, 0))],
        out_specs=pl.BlockSpec((tm, LANES), lambda i: (i, 0)),
        out_shape=jax.ShapeDtypeStruct((t, LANES), F32),
        compiler_params=_params("parallel"),
        name="moe_positions",
    )(meta, offs)


def _row_copy(src, dst, sem):
    return pltpu.make_async_copy(src, dst, sem)


def _dispatch_kernel(pos_ref, hn_ref, xs_in_ref, xs_ref, sem, *, tm):
    del xs_in_ref
    base = pl.program_id(0) * tm

    def issue(r, carry):
        for k in range(2):
            p = pos_ref[2 * (base + r) + k]
            _row_copy(hn_ref.at[pl.ds(r, 1), :], xs_ref.at[pl.ds(p, 1), :], sem).start()
        return carry

    lax.fori_loop(0, tm, issue, 0)

    def drain(r, carry):
        for k in range(2):
            _row_copy(hn_ref.at[pl.ds(0, 1), :], xs_ref.at[pl.ds(0, 1), :], sem).wait()
        return carry

    lax.fori_loop(0, tm, drain, 0)


def dispatch(pos, hn, n_rows, tm):
    t, d = hn.shape
    xs0 = jnp.zeros((n_rows, d), hn.dtype)
    return pl.pallas_call(
        functools.partial(_dispatch_kernel, tm=tm),
        grid_spec=pltpu.PrefetchScalarGridSpec(
            num_scalar_prefetch=1,
            grid=(t // tm,),
            in_specs=[pl.BlockSpec((tm, d), lambda i, pos: (i, 0)), pl.BlockSpec(memory_space=pl.ANY)],
            out_specs=pl.BlockSpec(memory_space=pl.ANY),
            scratch_shapes=[pltpu.SemaphoreType.DMA(())],
        ),
        out_shape=jax.ShapeDtypeStruct((n_rows, d), hn.dtype),
        input_output_aliases={2: 0},
        compiler_params=_params("arbitrary"),
        name="moe_dispatch",
    )(pos, hn, xs0)


def _expert_kernel(te_ref, nu_ref, xs_ref, w1_ref, w3_ref, w2_ref, o_ref):
    i = pl.program_id(0)

    @pl.when(i < nu_ref[0])
    def _():
        x = xs_ref[...].astype(BF)
        a = jnp.dot(x, w1_ref[...], preferred_element_type=F32)
        c = jnp.dot(x, w3_ref[...], preferred_element_type=F32)
        hid = (a * (1.0 / (1.0 + jnp.exp(-a))) * c).astype(BF)
        o_ref[...] = jnp.dot(hid, w2_ref[...], preferred_element_type=F32)

    @pl.when(i >= nu_ref[0])
    def _():
        o_ref[...] = jnp.zeros_like(o_ref)


def expert_mlp(tile_expert, n_used, xs, w1, w3, w2):
    n_rows, d = xs.shape
    f = w1.shape[2]
    return pl.pallas_call(
        _expert_kernel,
        grid_spec=pltpu.PrefetchScalarGridSpec(
            num_scalar_prefetch=2,
            grid=(n_rows // EXPERT_TILE,),
            in_specs=[
                pl.BlockSpec((EXPERT_TILE, d), lambda i, te, nu: (i, 0)),
                pl.BlockSpec((None, d, f), lambda i, te, nu: (te[i], 0, 0)),
                pl.BlockSpec((None, d, f), lambda i, te, nu: (te[i], 0, 0)),
                pl.BlockSpec((None, f, d), lambda i, te, nu: (te[i], 0, 0)),
            ],
            out_specs=pl.BlockSpec((EXPERT_TILE, d), lambda i, te, nu: (i, 0)),
        ),
        out_shape=jax.ShapeDtypeStruct((n_rows, d), F32),
        compiler_params=_params("arbitrary"),
        name="moe_experts",
    )(tile_expert, n_used, xs, w1, w3, w2)


def _combine_kernel(pos_ref, x_ref, meta_ref, gf_ref, ys_ref, o_ref, buf_ref, sem, *, tm, final_norm):
    base = pl.program_id(0) * tm

    def issue(r, carry):
        for k in range(2):
            p = pos_ref[2 * (base + r) + k]
            _row_copy(ys_ref.at[pl.ds(p, 1), :], buf_ref.at[k, pl.ds(r, 1), :], sem).start()
        return carry

    lax.fori_loop(0, tm, issue, 0)

    def drain(r, carry):
        for k in range(2):
            _row_copy(ys_ref.at[pl.ds(0, 1), :], buf_ref.at[k, pl.ds(0, 1), :], sem).wait()
        return carry

    lax.fori_loop(0, tm, drain, 0)
    meta = meta_ref[...]
    y = x_ref[...] + meta[:, 4:5] * buf_ref[0] + meta[:, 5:6] * buf_ref[1]
    if final_norm:
        y = _rms(y, gf_ref[...])
    o_ref[...] = y


def combine(pos, x, meta, g_final, ys, tm, final_norm):
    t, d = x.shape
    return pl.pallas_call(
        functools.partial(_combine_kernel, tm=tm, final_norm=final_norm),
        grid_spec=pltpu.PrefetchScalarGridSpec(
            num_scalar_prefetch=1,
            grid=(t // tm,),
            in_specs=[
                pl.BlockSpec((tm, d), lambda i, pos: (i, 0)),
                pl.BlockSpec((tm, LANES), lambda i, pos: (i, 0)),
                pl.BlockSpec((1, d), lambda i, pos: (0, 0)),
                pl.BlockSpec(memory_space=pl.ANY),
            ],
            out_specs=pl.BlockSpec((tm, d), lambda i, pos: (i, 0)),
            scratch_shapes=[pltpu.VMEM((2, tm, d), F32), pltpu.SemaphoreType.DMA(())],
        ),
        out_shape=jax.ShapeDtypeStruct((t, d), F32),
        compiler_params=_params("arbitrary"),
        name="moe_combine",
    )(pos, x, meta, g_final, ys)


def _alibi_slopes():
    n = A_HEADS + B_HEADS
    s = 2.0 ** (-8.0 * np.arange(1, n + 1) / n)
    is_b = (np.arange(n) % 3) == 2
    return jnp.asarray(s[~is_b], F32), jnp.asarray(s[is_b], F32)


def hier_moe(x, g_ffn, w_group, b_group, w_router, b_router, w1, w3, w2, g_final, final_norm):
    t, d = x.shape
    wr = jnp.concatenate([w_group, w_router.transpose(1, 0, 2).reshape(d, N_EXPERTS),
                          jnp.zeros((d, LANES - N_GROUPS - N_EXPERTS), F32)], axis=1)
    br = jnp.concatenate([b_group, b_router.reshape(N_EXPERTS),
                          jnp.zeros((LANES - N_GROUPS - N_EXPERTS,), F32)])[None, :]
    whi = wr.astype(BF)
    wlo = (wr - whi.astype(F32)).astype(BF)
    hn, meta, counts = router(x, g_ffn[None, :], whi, wlo, br, tm=512)

    cnt = counts[0, :N_EXPERTS].astype(jnp.int32)
    ntile = (cnt + EXPERT_TILE - 1) // EXPERT_TILE
    tend = jnp.cumsum(ntile)
    tstart = tend - ntile
    n_used = tend[-1]
    n_tiles = (2 * t) // EXPERT_TILE + N_EXPERTS
    tile_ids = jnp.minimum(jnp.arange(n_tiles, dtype=jnp.int32), n_used - 1)
    tile_expert = jnp.sum(tile_ids[:, None] >= tend[None, :], axis=1).astype(jnp.int32)
    offs = jnp.zeros((1, LANES), F32).at[0, :N_EXPERTS].set((tstart * EXPERT_TILE).astype(F32))

    pos = positions(meta, offs, tm=512)[:, :2].astype(jnp.int32).reshape(-1)
    xs = dispatch(pos, hn, n_tiles * EXPERT_TILE, tm=256)
    ys = expert_mlp(tile_expert, n_used.reshape(1).astype(jnp.int32), xs,
                    w1.reshape(N_EXPERTS, d, EXPERT_FF).astype(BF),
                    w3.reshape(N_EXPERTS, d, EXPERT_FF).astype(BF),
                    w2.reshape(N_EXPERTS, EXPERT_FF, d).astype(BF))
    return combine(pos, x, meta, g_final[None, :], ys, tm=256, final_norm=final_norm)


def kernel(x, mem, g_mix, w_in, w_out, pool_w, pool_scale, g_mem, g_memkv, w_mq, w_mk, w_mv, w_mo,
           g_ffn, w_group, b_group, w_router, b_router, w1, w3, w2, g_final):
    b, s, d = x.shape
    t = b * s
    depth = w_in.shape[0]
    slopes_a, slopes_b = _alibi_slopes()
    kaug = moba_key_features(s)
    n_a = 3 * A_WIDTH
    n_b = 3 * B_WIDTH + C_WIDTH
    cs_a = jnp.concatenate([jnp.full((A_WIDTH,), SCALE, F32), jnp.ones((2 * A_WIDTH,), F32)])[None, :]
    cs_b = jnp.concatenate([jnp.full((B_WIDTH,), SCALE, F32), jnp.ones((n_b - B_WIDTH,), F32)])[None, :]
    ones_kv = jnp.ones((1, 2 * MEM_WIDTH), F32)
    xf = x.reshape(t, d)
    memf = mem.reshape(-1, d)
    for l in range(depth):
        w_in_l = w_in[l].astype(BF)
        gl = g_mix[l][None, :]
        pa = norm_matmul(xf, gl, w_in_l[:, :n_a], cs_a, BF, tm=1024, tn=1024).reshape(b, s, n_a)
        pb = norm_matmul(xf, gl, w_in_l[:, n_a:], cs_b, F32, tm=1024, tn=1024).reshape(b, s, n_b)
        oa = moba_attention(pa, kaug, slopes_a, bq=256)
        ob = dilated_attention(pb, slopes_b)
        oc = pool_mixer(pb, pool_w[l].astype(BF), pool_scale[l][None, :], tm=512)
        xf = out_proj(xf, oa.reshape(t, A_WIDTH), ob.reshape(t, B_WIDTH), oc.reshape(t, C_WIDTH),
                      w_out[l].astype(BF), tm=256)
        wkv = jnp.concatenate([w_mk[l], w_mv[l]], axis=1).astype(BF)
        kv = norm_matmul(memf, g_memkv[l][None, :], wkv, ones_kv, BF, tm=memf.shape[0], tn=2 * MEM_WIDTH)
        xf = cross_attention(xf.reshape(b, s, d), g_mem[l][None, :], w_mq[l].astype(BF),
                             kv.reshape(b, -1, 2 * MEM_WIDTH), w_mo[l].astype(BF), tm=256).reshape(t, d)
        xf = hier_moe(xf, g_ffn[l], w_group[l], b_group[l], w_router[l], b_router[l],
                      w1[l], w3[l], w2[l], g_final, final_norm=(l == depth - 1))
    return xf.reshape(b, s, d)
```

```python
import functools

import numpy as np
import jax
import jax.numpy as jnp
from jax import lax
from jax.experimental import pallas as pl
from jax.experimental.pallas import tpu as pltpu

D_MODEL = 2048
HEAD_DIM = 128
A_HEADS = 8
B_HEADS = 4
POOL_GROUPS = 4
POOL_WINDOWS = (2, 4, 8, 16)
A_WIDTH = A_HEADS * HEAD_DIM
B_WIDTH = B_HEADS * HEAD_DIM
C_WIDTH = POOL_GROUPS * HEAD_DIM
MOBA_BLOCK = 256
MOBA_TOPK = 3
DIL_PAIRS = ((128, 1), (512, 4), (2048, 16))
DIL_OFFSETS = 128
DIL_UNIT = 2048
MEM_HEADS = 4
MEM_WIDTH = MEM_HEADS * HEAD_DIM
N_GROUPS = 4
EXPERTS_PER_GROUP = 8
N_EXPERTS = N_GROUPS * EXPERTS_PER_GROUP
EXPERT_FF = D_MODEL // 4
RMS_EPS = 1e-6
SCALE = HEAD_DIM ** -0.5
LOG2E = 1.4426950408889634

LANES = 128
MOBA_SLOTS = 32
EXPERT_TILE = 256
NEG = -(2.0 ** 100)
VMEM_LIMIT = 56 * 1024 * 1024

BF = jnp.bfloat16
F32 = jnp.float32
NT_DIMS = (((1,), (1,)), ((), ()))


def _params(*sem):
    return pltpu.CompilerParams(dimension_semantics=sem, vmem_limit_bytes=VMEM_LIMIT)


def _rms(x, g):
    return x * lax.rsqrt(jnp.mean(x * x, axis=-1, keepdims=True) + RMS_EPS) * g


def _split3(x):
    hi = x.astype(BF).astype(F32)
    r1 = x - hi
    mid = r1.astype(BF).astype(F32)
    lo = (r1 - mid).astype(BF).astype(F32)
    return hi, mid, lo


def _norm_matmul_kernel(x_ref, g_ref, w_ref, cs_ref, o_ref, hn_ref):
    @pl.when(pl.program_id(1) == 0)
    def _():
        hn_ref[...] = _rms(x_ref[...], g_ref[...]).astype(BF)

    acc = jnp.dot(hn_ref[...], w_ref[...], preferred_element_type=F32)
    o_ref[...] = (acc * cs_ref[...]).astype(o_ref.dtype)


def norm_matmul(x, g, w, colscale, out_dtype, tm, tn):
    m, d = x.shape
    n = w.shape[1]
    return pl.pallas_call(
        _norm_matmul_kernel,
        grid=(m // tm, n // tn),
        in_specs=[
            pl.BlockSpec((tm, d), lambda i, j: (i, 0)),
            pl.BlockSpec((1, d), lambda i, j: (0, 0)),
            pl.BlockSpec((d, tn), lambda i, j: (0, j)),
            pl.BlockSpec((1, tn), lambda i, j: (0, j)),
        ],
        out_specs=pl.BlockSpec((tm, tn), lambda i, j: (i, j)),
        out_shape=jax.ShapeDtypeStruct((m, n), out_dtype),
        scratch_shapes=[pltpu.VMEM((tm, d), BF)],
        compiler_params=_params("parallel", "arbitrary"),
        name="norm_matmul",
    )(x, g, w, colscale)


MOBA_STEP = 2 * MOBA_BLOCK


MOBA_HEADS_PER_STEP = 4


def _moba_setup_head(k, v_ref, kaug_ref, kfull_ref, vt_ref, kmean_ref, cols, nblk):
    kfull_ref[:, :HEAD_DIM] = k
    kfull_ref[:, HEAD_DIM:] = kaug_ref[...]
    km = jnp.mean(k.astype(F32).reshape(nblk, MOBA_BLOCK, HEAD_DIM), axis=1)
    if nblk < MOBA_SLOTS:
        km = jnp.concatenate([km, jnp.zeros((MOBA_SLOTS - nblk, HEAD_DIM), F32)], axis=0)
    pad = jnp.zeros((LANES - 3 * MOBA_SLOTS, HEAD_DIM), F32)
    kmean_ref[...] = jnp.concatenate([km, km, km, pad], axis=0).astype(BF)

    def transpose_v(jj, carry):
        start = pl.multiple_of(jj * MOBA_STEP, MOBA_STEP)
        vt_ref[jj] = v_ref[pl.ds(start, MOBA_STEP), cols].astype(F32).T.astype(BF)
        return carry

    lax.fori_loop(0, nblk // 2, transpose_v, 0)


def _moba_query_features(qt, kmean, slope, i):
    bq = qt.shape[1]
    gate = jnp.dot(kmean, qt, preferred_element_type=F32)
    feat = lax.broadcasted_iota(jnp.int32, (LANES, bq), 0)
    kblk = feat & (MOBA_SLOTS - 1)
    is_feat = feat < 3 * MOBA_SLOTS
    past = jnp.logical_and(kblk < i, is_feat)
    featf = feat.astype(F32)
    kblkf = kblk.astype(F32)
    g = jnp.where(past, gate, -jnp.inf)
    sel = jnp.zeros((LANES, bq), jnp.bool_)
    for _ in range(MOBA_TOPK):
        mx = jnp.max(g, axis=0, keepdims=True)
        first = jnp.min(jnp.where(g == mx, featf, 1e9), axis=0, keepdims=True)
        pick = kblkf == first
        sel = jnp.logical_or(sel, pick)
        g = jnp.where(pick, -jnp.inf, g)
    sel = jnp.logical_and(sel, past)
    blkdist = (i - kblk).astype(F32) * float(MOBA_BLOCK)
    own = jnp.logical_and(kblk == i, is_feat)
    bias = jnp.where(sel, -slope * blkdist, jnp.where(own, 0.0, NEG))
    b_hi, b_mid, b_lo = _split3(bias)
    s_hi, s_mid, s_lo = _split3(jnp.full((LANES, bq), slope, F32))
    qaug = jnp.where(feat < MOBA_SLOTS, b_hi,
           jnp.where(feat < 2 * MOBA_SLOTS, b_mid,
           jnp.where(feat < 3 * MOBA_SLOTS, b_lo,
           jnp.where(feat == 3 * MOBA_SLOTS, s_hi,
           jnp.where(feat == 3 * MOBA_SLOTS + 1, s_mid,
           jnp.where(feat == 3 * MOBA_SLOTS + 2, s_lo, 0.0))))))
    return jnp.concatenate([qt, qaug.astype(BF)], axis=0)


def _moba_kernel(slopes_ref, q_ref, k_ref, v_ref, kaug_ref, o_ref, kfull_ref, vt_ref, kmean_ref, *, nblk):
    hg = pl.program_id(1)
    i = pl.program_id(2)
    bq = MOBA_BLOCK
    heads = range(MOBA_HEADS_PER_STEP)
    cols = [slice(g * HEAD_DIM, (g + 1) * HEAD_DIM) for g in heads]

    @pl.when(i == 0)
    def _():
        for g in heads:
            _moba_setup_head(k_ref[:, cols[g]], v_ref, kaug_ref, kfull_ref.at[g], vt_ref.at[g],
                             kmean_ref.at[g], cols[g], nblk)

    q_full = []
    for g in heads:
        qt = q_ref[:, cols[g]].astype(F32).T.astype(BF)
        slope = slopes_ref[hg * MOBA_HEADS_PER_STEP + g] * LOG2E
        q_full.append(_moba_query_features(qt, kmean_ref[g], slope, i))

    def step(jj, carry, diagonal):
        start = pl.multiple_of(jj * MOBA_STEP, MOBA_STEP)
        scores = []
        for g in heads:
            kj = kfull_ref[g, pl.ds(start, MOBA_STEP), :]
            s = jnp.dot(kj, q_full[g], preferred_element_type=F32)
            if diagonal:
                kpos = start + lax.broadcasted_iota(jnp.int32, (MOBA_STEP, bq), 0)
                qpos = i * bq + lax.broadcasted_iota(jnp.int32, (MOBA_STEP, bq), 1)
                s = jnp.where(qpos >= kpos, s, NEG)
            scores.append(s)
        probs = []
        for g in heads:
            m, l, acc = carry[g]
            m_new = jnp.maximum(m, jnp.max(scores[g], axis=0, keepdims=True))
            alpha = jnp.exp2(m - m_new)
            p = jnp.exp2(scores[g] - m_new)
            l = alpha * l + jnp.sum(p, axis=0, keepdims=True)
            probs.append((m_new, l, alpha, p.astype(BF)))
        out = []
        for g in heads:
            m_new, l, alpha, p = probs[g]
            acc = alpha * carry[g][2] + jnp.dot(vt_ref[g, jj], p, preferred_element_type=F32)
            out.append((m_new, l, acc))
        return tuple(out)

    init = (jnp.full((1, bq), -jnp.inf, F32), jnp.zeros((1, bq), F32), jnp.zeros((HEAD_DIM, bq), F32))
    carry = lax.fori_loop(0, i // 2, lambda jj, c: step(jj, c, False), tuple(init for _ in heads))
    carry = step(i // 2, carry, True)
    for g in heads:
        _, l, acc = carry[g]
        o_ref[:, cols[g]] = (acc / l).T.astype(o_ref.dtype)


def moba_attention(pa, kaug, slopes):
    b, s, _ = pa.shape
    nblk = s // MOBA_BLOCK
    hps = MOBA_HEADS_PER_STEP
    groups = A_HEADS // hps
    assert s % MOBA_STEP == 0 and nblk <= MOBA_SLOTS
    return pl.pallas_call(
        functools.partial(_moba_kernel, nblk=nblk),
        grid=(b, groups, nblk),
        in_specs=[
            pl.BlockSpec(memory_space=pltpu.SMEM),
            pl.BlockSpec((None, MOBA_BLOCK, hps * HEAD_DIM), lambda bi, h, i: (bi, i, h)),
            pl.BlockSpec((None, s, hps * HEAD_DIM), lambda bi, h, i: (bi, 0, groups + h),
                         pipeline_mode=pl.Buffered(1)),
            pl.BlockSpec((None, s, hps * HEAD_DIM), lambda bi, h, i: (bi, 0, 2 * groups + h),
                         pipeline_mode=pl.Buffered(1)),
            pl.BlockSpec((s, LANES), lambda bi, h, i: (0, 0), pipeline_mode=pl.Buffered(1)),
        ],
        out_specs=pl.BlockSpec((None, MOBA_BLOCK, hps * HEAD_DIM), lambda bi, h, i: (bi, i, h)),
        out_shape=jax.ShapeDtypeStruct((b, s, A_WIDTH), BF),
        scratch_shapes=[
            pltpu.VMEM((hps, s, 2 * HEAD_DIM), BF),
            pltpu.VMEM((hps, nblk // 2, HEAD_DIM, MOBA_STEP), BF),
            pltpu.VMEM((hps, LANES, HEAD_DIM), BF),
        ],
        compiler_params=_params("parallel", "parallel", "arbitrary"),
        name="moba_attention",
    )(slopes, pa, pa, pa, kaug)


def moba_key_features(s):
    t = np.arange(s)
    f = np.arange(LANES)
    onehot = ((f[None, :] % MOBA_SLOTS) == (t[:, None] // MOBA_BLOCK)) & (f[None, :] < 3 * MOBA_SLOTS)
    ramp = ((f[None, :] >= 3 * MOBA_SLOTS) & (f[None, :] < 3 * MOBA_SLOTS + 3)) * (t[:, None] % MOBA_BLOCK)
    return jnp.asarray(onehot.astype(np.float32) + ramp.astype(np.float32), BF)


def _dil_kernel(slopes_ref, q_ref, kc_ref, kp_ref, vc_ref, vp_ref, o_ref, acc_ref, m_ref, l_ref):
    h = pl.program_id(1)
    u = pl.program_id(2)
    slope = slopes_ref[h] * LOG2E
    n = DIL_OFFSETS

    qi = lax.broadcasted_iota(jnp.int32, (n, 2 * n), 0)
    ki = lax.broadcasted_iota(jnp.int32, (n, 2 * n), 1)
    steps = qi + n - ki
    valid = jnp.logical_and(steps >= 0, steps <= n)
    stepsf = steps.astype(F32)
    first_ok = jnp.logical_or(ki >= n, u > 0)

    def keys(cur_ref, prev_ref, start, blk, dil):
        if blk > 0:
            return cur_ref[pl.ds(start - n * dil, 2 * n, stride=dil), :].astype(BF)
        before = prev_ref[pl.ds(DIL_UNIT - n * dil + start, n, stride=dil), :]
        return jnp.concatenate([before, cur_ref[pl.ds(start, n, stride=dil), :]], axis=0).astype(BF)

    for gi, (window, dil) in enumerate(DIL_PAIRS):
        nbk = DIL_UNIT // (dil * n)
        pen = jnp.where(valid, -slope * float(dil) * stepsf, NEG)
        blocks = [(blk * n * dil + r, blk) for r in range(dil) for blk in range(nbk)]
        rows = lambda start: pl.ds(start, n, stride=dil)
        scores = []
        for start, blk in blocks:
            q = q_ref[rows(start), :].astype(BF)
            s = lax.dot_general(q, keys(kc_ref, kp_ref, start, blk, dil), NT_DIMS,
                                preferred_element_type=F32) + pen
            scores.append(s if blk > 0 else jnp.where(first_ok, s, NEG))
        probs = []
        for (start, blk), s in zip(blocks, scores):
            m = jnp.max(s, axis=1, keepdims=True)
            p = jnp.exp2(s - m)
            l = jnp.sum(p, axis=1, keepdims=True)
            m_ref.at[gi][rows(start), :] = jnp.broadcast_to(m, (n, HEAD_DIM))
            l_ref.at[gi][rows(start), :] = jnp.broadcast_to(l, (n, HEAD_DIM))
            probs.append(p.astype(BF))
        for (start, blk), p in zip(blocks, probs):
            acc_ref.at[gi][rows(start), :] = jnp.dot(p, keys(vc_ref, vp_ref, start, blk, dil),
                                                     preferred_element_type=F32)

    m_all = jnp.maximum(jnp.maximum(m_ref[0], m_ref[1]), m_ref[2])
    num = jnp.zeros((DIL_UNIT, HEAD_DIM), F32)
    den = jnp.zeros((DIL_UNIT, HEAD_DIM), F32)
    for gi in range(len(DIL_PAIRS)):
        w = jnp.exp2(m_ref[gi] - m_all)
        num = num + w * acc_ref[gi]
        den = den + w * l_ref[gi]
    o_ref[...] = (num / den).astype(o_ref.dtype)


def dilated_attention(pb, slopes):
    b, s, _ = pb.shape
    assert s % DIL_UNIT == 0
    cur = lambda off: (lambda bi, h, u: (bi, u, off + h))
    prev = lambda off: (lambda bi, h, u: (bi, jnp.maximum(u - 1, 0), off + h))
    blk = (None, DIL_UNIT, HEAD_DIM)
    return pl.pallas_call(
        _dil_kernel,
        grid=(b, B_HEADS, s // DIL_UNIT),
        in_specs=[
            pl.BlockSpec(memory_space=pltpu.SMEM),
            pl.BlockSpec(blk, cur(0)),
            pl.BlockSpec(blk, cur(B_HEADS)),
            pl.BlockSpec(blk, prev(B_HEADS)),
            pl.BlockSpec(blk, cur(2 * B_HEADS)),
            pl.BlockSpec(blk, prev(2 * B_HEADS)),
        ],
        out_specs=pl.BlockSpec(blk, cur(0)),
        out_shape=jax.ShapeDtypeStruct((b, s, B_WIDTH), BF),
        scratch_shapes=[
            pltpu.VMEM((3, DIL_UNIT, HEAD_DIM), F32),
            pltpu.VMEM((3, DIL_UNIT, HEAD_DIM), F32),
            pltpu.VMEM((3, DIL_UNIT, HEAD_DIM), F32),
        ],
        compiler_params=_params("parallel", "parallel", "arbitrary"),
        name="dilated_attention",
    )(slopes, pb, pb, pb, pb, pb)


POOL_HALO = 16


def _pool_kernel(u_ref, halo_ref, w_ref, sc_ref, o_ref, ext_ref, *, tm):
    i = pl.program_id(1)
    halo = halo_ref[...]
    ext_ref[:POOL_HALO, :] = jnp.where(i > 0, halo, jnp.zeros_like(halo))
    ext_ref[POOL_HALO:, :] = u_ref[...]
    t = i * tm + lax.broadcasted_iota(jnp.int32, (tm, 1), 0)
    for gi, win in enumerate(POOL_WINDOWS):
        cols = slice(gi * HEAD_DIM, (gi + 1) * HEAD_DIM)
        tot = ext_ref[POOL_HALO:, cols]
        for back in range(1, win):
            tot = tot + ext_ref[pl.ds(POOL_HALO - back, tm), cols]
        cnt = jnp.minimum(t + 1, win).astype(F32)
        pooled = (tot / cnt - ext_ref[POOL_HALO:, cols]).astype(BF)
        y = jnp.dot(pooled, w_ref[gi], preferred_element_type=F32)
        o_ref[:, cols] = (y * sc_ref[:, cols]).astype(o_ref.dtype)


def pool_mixer(pb, pool_w, pool_scale, tm):
    b, s, _ = pb.shape
    ucol = 3 * B_WIDTH // C_WIDTH
    return pl.pallas_call(
        functools.partial(_pool_kernel, tm=tm),
        grid=(b, s // tm),
        in_specs=[
            pl.BlockSpec((None, tm, C_WIDTH), lambda bi, i: (bi, i, ucol)),
            pl.BlockSpec((None, POOL_HALO, C_WIDTH),
                         lambda bi, i: (bi, jnp.maximum(i * (tm // POOL_HALO) - 1, 0), ucol)),
            pl.BlockSpec((POOL_GROUPS, HEAD_DIM, HEAD_DIM), lambda bi, i: (0, 0, 0)),
            pl.BlockSpec((1, C_WIDTH), lambda bi, i: (0, 0)),
        ],
        out_specs=pl.BlockSpec((None, tm, C_WIDTH), lambda bi, i: (bi, i, 0)),
        out_shape=jax.ShapeDtypeStruct((b, s, C_WIDTH), BF),
        scratch_shapes=[pltpu.VMEM((tm + POOL_HALO, C_WIDTH), F32)],
        compiler_params=_params("parallel", "parallel"),
        name="pool_mixer",
    )(pb, pb, pool_w, pool_scale)


def _out_proj_kernel(x_ref, oa_ref, ob_ref, oc_ref, w_ref, o_ref):
    acc = jnp.dot(oa_ref[...], w_ref[:A_WIDTH, :], preferred_element_type=F32)
    acc += jnp.dot(ob_ref[...], w_ref[A_WIDTH:A_WIDTH + B_WIDTH, :], preferred_element_type=F32)
    acc += jnp.dot(oc_ref[...], w_ref[A_WIDTH + B_WIDTH:, :], preferred_element_type=F32)
    o_ref[...] = x_ref[...] + acc


def out_proj(x, oa, ob, oc, w, tm):
    t, d = x.shape
    return pl.pallas_call(
        _out_proj_kernel,
        grid=(t // tm,),
        in_specs=[
            pl.BlockSpec((tm, d), lambda i: (i, 0)),
            pl.BlockSpec((tm, A_WIDTH), lambda i: (i, 0)),
            pl.BlockSpec((tm, B_WIDTH), lambda i: (i, 0)),
            pl.BlockSpec((tm, C_WIDTH), lambda i: (i, 0)),
            pl.BlockSpec(w.shape, lambda i: (0, 0)),
        ],
        out_specs=pl.BlockSpec((tm, d), lambda i: (i, 0)),
        out_shape=jax.ShapeDtypeStruct((t, d), F32),
        compiler_params=_params("parallel"),
        name="out_proj",
    )(x, oa, ob, oc, w)


def _cross_kernel(x_ref, g_ref, wq_ref, kv_ref, wo_ref, o_ref):
    x = x_ref[...]
    hn = _rms(x, g_ref[...]).astype(BF)
    q = (jnp.dot(hn, wq_ref[...], preferred_element_type=F32) * SCALE).astype(BF)
    outs = []
    for hd in range(MEM_HEADS):
        cols = slice(hd * HEAD_DIM, (hd + 1) * HEAD_DIM)
        k = kv_ref[:, cols]
        v = kv_ref[:, MEM_WIDTH + hd * HEAD_DIM:MEM_WIDTH + (hd + 1) * HEAD_DIM]
        s = lax.dot_general(q[:, cols], k, NT_DIMS, preferred_element_type=F32)
        m = jnp.max(s, axis=1, keepdims=True)
        p = jnp.exp(s - m)
        l = jnp.sum(p, axis=1, keepdims=True)
        outs.append((jnp.dot(p.astype(BF), v, preferred_element_type=F32) / l).astype(BF))
    o = jnp.concatenate(outs, axis=1)
    o_ref[...] = x + jnp.dot(o, wo_ref[...], preferred_element_type=F32)


def cross_attention(x3, g, wq, kv, wo, tm):
    b, s, d = x3.shape
    mem_len = kv.shape[1]
    return pl.pallas_call(
        _cross_kernel,
        grid=(b, s // tm),
        in_specs=[
            pl.BlockSpec((None, tm, d), lambda bi, i: (bi, i, 0)),
            pl.BlockSpec((1, d), lambda bi, i: (0, 0)),
            pl.BlockSpec(wq.shape, lambda bi, i: (0, 0)),
            pl.BlockSpec((None, mem_len, 2 * MEM_WIDTH), lambda bi, i: (bi, 0, 0)),
            pl.BlockSpec(wo.shape, lambda bi, i: (0, 0)),
        ],
        out_specs=pl.BlockSpec((None, tm, d), lambda bi, i: (bi, i, 0)),
        out_shape=jax.ShapeDtypeStruct((b, s, d), F32),
        compiler_params=_params("parallel", "parallel"),
        name="cross_attention",
    )(x3, g, wq, kv, wo)


def _router_kernel(x_ref, g_ref, whi_ref, wlo_ref, b_ref, hn_ref, meta_ref, cnt_ref, base_ref, *, tm):
    @pl.when(pl.program_id(0) == 0)
    def _():
        base_ref[...] = jnp.zeros_like(base_ref)

    hn = _rms(x_ref[...], g_ref[...])
    hn_ref[...] = hn
    hi = hn.astype(BF)
    lo = (hn - hi.astype(F32)).astype(BF)
    logits = (jnp.dot(hi, whi_ref[...], preferred_element_type=F32)
              + jnp.dot(lo, whi_ref[...], preferred_element_type=F32)
              + jnp.dot(hi, wlo_ref[...], preferred_element_type=F32)) + b_ref[...]
    lane = lax.broadcasted_iota(jnp.int32, (tm, LANES), 1)
    lanef = lane.astype(F32)

    def first_argmax(vals):
        mx = jnp.max(vals, axis=1, keepdims=True)
        idx = jnp.min(jnp.where(vals == mx, lanef, 1e9), axis=1, keepdims=True)
        return mx, idx

    is_group = lane < N_GROUPS
    gmax, gsel = first_argmax(jnp.where(is_group, logits, -jnp.inf))
    gsum = jnp.sum(jnp.where(is_group, jnp.exp(logits - gmax), 0.0), axis=1, keepdims=True)
    gw = 1.0 / gsum
    lane_group = ((lane - N_GROUPS) // EXPERTS_PER_GROUP).astype(F32)
    in_group = jnp.logical_and(jnp.logical_and(lane >= N_GROUPS, lane < N_GROUPS + N_EXPERTS),
                               lane_group == gsel)
    el = jnp.where(in_group, logits, -jnp.inf)
    v1, i1 = first_argmax(el)
    v2, i2 = first_argmax(jnp.where(lanef == i1, -jnp.inf, el))
    t = jnp.exp(v2 - v1)
    w1 = gw / (1.0 + t)
    w2 = gw * t / (1.0 + t)
    e1 = i1 - N_GROUPS
    e2 = i2 - N_GROUPS
    oh1 = lanef == e1
    oh2 = lanef == e2
    cnt = jnp.where(jnp.logical_or(oh1, oh2), 1.0, 0.0)
    r = lax.broadcasted_iota(jnp.int32, (tm, tm), 0)
    c = lax.broadcasted_iota(jnp.int32, (tm, tm), 1)
    lower = jnp.where(r > c, 1.0, 0.0).astype(BF)
    before = jnp.dot(lower, cnt.astype(BF), preferred_element_type=F32) + base_ref[...]
    rank1 = jnp.sum(jnp.where(oh1, before, 0.0), axis=1, keepdims=True)
    rank2 = jnp.sum(jnp.where(oh2, before, 0.0), axis=1, keepdims=True)
    base_ref[...] += jnp.sum(cnt, axis=0, keepdims=True)
    cnt_ref[...] = base_ref[...]
    meta_ref[...] = jnp.where(lane == 0, e1, jnp.where(lane == 1, e2, jnp.where(lane == 2, rank1,
                    jnp.where(lane == 3, rank2, jnp.where(lane == 4, w1, jnp.where(lane == 5, w2, 0.0))))))


def router(x, g, whi, wlo, bias, tm):
    t, d = x.shape
    return pl.pallas_call(
        functools.partial(_router_kernel, tm=tm),
        grid=(t // tm,),
        in_specs=[
            pl.BlockSpec((tm, d), lambda i: (i, 0)),
            pl.BlockSpec((1, d), lambda i: (0, 0)),
            pl.BlockSpec((d, LANES), lambda i: (0, 0)),
            pl.BlockSpec((d, LANES), lambda i: (0, 0)),
            pl.BlockSpec((1, LANES), lambda i: (0, 0)),
        ],
        out_specs=[
            pl.BlockSpec((tm, d), lambda i: (i, 0)),
            pl.BlockSpec((tm, LANES), lambda i: (i, 0)),
            pl.BlockSpec((1, LANES), lambda i: (0, 0)),
        ],
        out_shape=[
            jax.ShapeDtypeStruct((t, d), F32),
            jax.ShapeDtypeStruct((t, LANES), F32),
            jax.ShapeDtypeStruct((1, LANES), F32),
        ],
        scratch_shapes=[pltpu.VMEM((1, LANES), F32)],
        compiler_params=_params("arbitrary"),
        name="moe_router",
    )(x, g, whi, wlo, bias)


def _positions_kernel(meta_ref, offs_ref, o_ref, *, tm):
    meta = meta_ref[...]
    lane = lax.broadcasted_iota(jnp.int32, (tm, LANES), 1)
    lanef = lane.astype(F32)
    offs = offs_ref[...]
    pos1 = jnp.sum(jnp.where(lanef == meta[:, 0:1], offs, 0.0), axis=1, keepdims=True) + meta[:, 2:3]
    pos2 = jnp.sum(jnp.where(lanef == meta[:, 1:2], offs, 0.0), axis=1, keepdims=True) + meta[:, 3:4]
    o_ref[...] = jnp.where(lane == 0, pos1, jnp.where(lane == 1, pos2, 0.0))


def positions(meta, offs, tm):
    t = meta.shape[0]
    return pl.pallas_call(
        functools.partial(_positions_kernel, tm=tm),
        grid=(t // tm,),
        in_specs=[pl.BlockSpec((tm, LANES), lambda i: (i, 0)), pl.BlockSpec((1, LANES), lambda i: (0, 0))],
        out_specs=pl.BlockSpec((tm, LANES), lambda i: (i, 0)),
        out_shape=jax.ShapeDtypeStruct((t, LANES), F32),
        compiler_params=_params("parallel"),
        name="moe_positions",
    )(meta, offs)


def _row_copy(src, dst, sem):
    return pltpu.make_async_copy(src, dst, sem)


def _dispatch_kernel(pos_ref, hn_ref, xs_in_ref, xs_ref, sem, *, tm):
    del xs_in_ref
    base = pl.program_id(0) * tm

    def issue(r, carry):
        for k in range(2):
            p = pos_ref[2 * (base + r) + k]
            _row_copy(hn_ref.at[pl.ds(r, 1), :], xs_ref.at[pl.ds(p, 1), :], sem).start()
        return carry

    lax.fori_loop(0, tm, issue, 0)

    def drain(r, carry):
        for k in range(2):
            _row_copy(hn_ref.at[pl.ds(0, 1), :], xs_ref.at[pl.ds(0, 1), :], sem).wait()
        return carry

    lax.fori_loop(0, tm, drain, 0)


def dispatch(pos, hn, n_rows, tm):
    t, d = hn.shape
    xs0 = jnp.zeros((n_rows, d), hn.dtype)
    return pl.pallas_call(
        functools.partial(_dispatch_kernel, tm=tm),
        grid_spec=pltpu.PrefetchScalarGridSpec(
            num_scalar_prefetch=1,
            grid=(t // tm,),
            in_specs=[pl.BlockSpec((tm, d), lambda i, pos: (i, 0)), pl.BlockSpec(memory_space=pl.ANY)],
            out_specs=pl.BlockSpec(memory_space=pl.ANY),
            scratch_shapes=[pltpu.SemaphoreType.DMA(())],
        ),
        out_shape=jax.ShapeDtypeStruct((n_rows, d), hn.dtype),
        input_output_aliases={2: 0},
        compiler_params=_params("arbitrary"),
        name="moe_dispatch",
    )(pos, hn, xs0)


def _expert_kernel(te_ref, nu_ref, xs_ref, w1_ref, w3_ref, w2_ref, o_ref, b1_ref, b3_ref, b2_ref):
    i = pl.program_id(0)
    new_expert = jnp.logical_or(i == 0, te_ref[i] != te_ref[jnp.maximum(i - 1, 0)])

    @pl.when(new_expert)
    def _():
        b1_ref[...] = w1_ref[...].astype(BF)
        b3_ref[...] = w3_ref[...].astype(BF)
        b2_ref[...] = w2_ref[...].astype(BF)

    @pl.when(i < nu_ref[0])
    def _():
        x = xs_ref[...].astype(BF)
        a = jnp.dot(x, b1_ref[...], preferred_element_type=F32)
        c = jnp.dot(x, b3_ref[...], preferred_element_type=F32)
        hid = (a * (1.0 / (1.0 + jnp.exp(-a))) * c).astype(BF)
        o_ref[...] = jnp.dot(hid, b2_ref[...], preferred_element_type=F32)

    @pl.when(i >= nu_ref[0])
    def _():
        o_ref[...] = jnp.zeros_like(o_ref)


def expert_mlp(tile_expert, n_used, xs, w1, w3, w2):
    n_rows, d = xs.shape
    f = w1.shape[2]
    return pl.pallas_call(
        _expert_kernel,
        grid_spec=pltpu.PrefetchScalarGridSpec(
            num_scalar_prefetch=2,
            grid=(n_rows // EXPERT_TILE,),
            in_specs=[
                pl.BlockSpec((EXPERT_TILE, d), lambda i, te, nu: (i, 0)),
                pl.BlockSpec((None, d, f), lambda i, te, nu: (te[i], 0, 0)),
                pl.BlockSpec((None, d, f), lambda i, te, nu: (te[i], 0, 0)),
                pl.BlockSpec((None, f, d), lambda i, te, nu: (te[i], 0, 0)),
            ],
            out_specs=pl.BlockSpec((EXPERT_TILE, d), lambda i, te, nu: (i, 0)),
            scratch_shapes=[pltpu.VMEM((d, f), BF), pltpu.VMEM((d, f), BF), pltpu.VMEM((f, d), BF)],
        ),
        out_shape=jax.ShapeDtypeStruct((n_rows, d), F32),
        compiler_params=_params("arbitrary"),
        name="moe_experts",
    )(tile_expert, n_used, xs, w1, w3, w2)


def _combine_kernel(pos_ref, x_ref, meta_ref, gf_ref, ys_ref, o_ref, buf_ref, sem, *, tm, final_norm):
    base = pl.program_id(0) * tm

    def issue(r, carry):
        for k in range(2):
            p = pos_ref[2 * (base + r) + k]
            _row_copy(ys_ref.at[pl.ds(p, 1), :], buf_ref.at[k, pl.ds(r, 1), :], sem).start()
        return carry

    lax.fori_loop(0, tm, issue, 0)

    def drain(r, carry):
        for k in range(2):
            _row_copy(ys_ref.at[pl.ds(0, 1), :], buf_ref.at[k, pl.ds(0, 1), :], sem).wait()
        return carry

    lax.fori_loop(0, tm, drain, 0)
    meta = meta_ref[...]
    y = x_ref[...] + meta[:, 4:5] * buf_ref[0] + meta[:, 5:6] * buf_ref[1]
    if final_norm:
        y = _rms(y, gf_ref[...])
    o_ref[...] = y


def combine(pos, x, meta, g_final, ys, tm, final_norm):
    t, d = x.shape
    return pl.pallas_call(
        functools.partial(_combine_kernel, tm=tm, final_norm=final_norm),
        grid_spec=pltpu.PrefetchScalarGridSpec(
            num_scalar_prefetch=1,
            grid=(t // tm,),
            in_specs=[
                pl.BlockSpec((tm, d), lambda i, pos: (i, 0)),
                pl.BlockSpec((tm, LANES), lambda i, pos: (i, 0)),
                pl.BlockSpec((1, d), lambda i, pos: (0, 0)),
                pl.BlockSpec(memory_space=pl.ANY),
            ],
            out_specs=pl.BlockSpec((tm, d), lambda i, pos: (i, 0)),
            scratch_shapes=[pltpu.VMEM((2, tm, d), F32), pltpu.SemaphoreType.DMA(())],
        ),
        out_shape=jax.ShapeDtypeStruct((t, d), F32),
        compiler_params=_params("arbitrary"),
        name="moe_combine",
    )(pos, x, meta, g_final, ys)


def _alibi_slopes():
    n = A_HEADS + B_HEADS
    s = 2.0 ** (-8.0 * np.arange(1, n + 1) / n)
    is_b = (np.arange(n) % 3) == 2
    return jnp.asarray(s[~is_b], F32), jnp.asarray(s[is_b], F32)


def hier_moe(x, g_ffn, w_group, b_group, w_router, b_router, w1, w3, w2, layer, g_final, final_norm):
    t, d = x.shape
    wr = jnp.concatenate([w_group, w_router.transpose(1, 0, 2).reshape(d, N_EXPERTS),
                          jnp.zeros((d, LANES - N_GROUPS - N_EXPERTS), F32)], axis=1)
    br = jnp.concatenate([b_group, b_router.reshape(N_EXPERTS),
                          jnp.zeros((LANES - N_GROUPS - N_EXPERTS,), F32)])[None, :]
    whi = wr.astype(BF)
    wlo = (wr - whi.astype(F32)).astype(BF)
    hn, meta, counts = router(x, g_ffn[None, :], whi, wlo, br, tm=512)

    cnt = counts[0, :N_EXPERTS].astype(jnp.int32)
    ntile = (cnt + EXPERT_TILE - 1) // EXPERT_TILE
    tend = jnp.cumsum(ntile)
    tstart = tend - ntile
    n_used = tend[-1]
    n_tiles = (2 * t) // EXPERT_TILE + N_EXPERTS
    tile_ids = jnp.minimum(jnp.arange(n_tiles, dtype=jnp.int32), n_used - 1)
    tile_expert = jnp.sum(tile_ids[:, None] >= tend[None, :], axis=1).astype(jnp.int32)
    offs = jnp.zeros((1, LANES), F32).at[0, :N_EXPERTS].set((tstart * EXPERT_TILE).astype(F32))

    pos = positions(meta, offs, tm=512)[:, :2].astype(jnp.int32).reshape(-1)
    xs = dispatch(pos, hn, n_tiles * EXPERT_TILE, tm=256)
    ys = expert_mlp(tile_expert + layer * N_EXPERTS, n_used.reshape(1).astype(jnp.int32), xs, w1, w3, w2)
    return combine(pos, x, meta, g_final[None, :], ys, tm=256, final_norm=final_norm)


def kernel(x, mem, g_mix, w_in, w_out, pool_w, pool_scale, g_mem, g_memkv, w_mq, w_mk, w_mv, w_mo,
           g_ffn, w_group, b_group, w_router, b_router, w1, w3, w2, g_final):
    b, s, d = x.shape
    t = b * s
    depth = w_in.shape[0]
    slopes_a, slopes_b = _alibi_slopes()
    kaug = moba_key_features(s)
    n_a = 3 * A_WIDTH
    n_b = 3 * B_WIDTH + C_WIDTH
    cs_a = jnp.concatenate([jnp.full((A_WIDTH,), SCALE * LOG2E, F32), jnp.ones((2 * A_WIDTH,), F32)])[None, :]
    cs_b = jnp.concatenate([jnp.full((B_WIDTH,), SCALE * LOG2E, F32), jnp.ones((n_b - B_WIDTH,), F32)])[None, :]
    ones_kv = jnp.ones((1, 2 * MEM_WIDTH), F32)
    xf = x.reshape(t, d)
    memf = mem.reshape(-1, d)
    for l in range(depth):
        w_in_l = w_in[l].astype(BF)
        gl = g_mix[l][None, :]
        pa = norm_matmul(xf, gl, w_in_l[:, :n_a], cs_a, BF, tm=1024, tn=1024).reshape(b, s, n_a)
        pb = norm_matmul(xf, gl, w_in_l[:, n_a:], cs_b, F32, tm=1024, tn=1024).reshape(b, s, n_b)
        oa = moba_attention(pa, kaug, slopes_a)
        ob = dilated_attention(pb, slopes_b)
        oc = pool_mixer(pb, pool_w[l].astype(BF), pool_scale[l][None, :], tm=512)
        xf = out_proj(xf, oa.reshape(t, A_WIDTH), ob.reshape(t, B_WIDTH), oc.reshape(t, C_WIDTH),
                      w_out[l].astype(BF), tm=256)
        wkv = jnp.concatenate([w_mk[l], w_mv[l]], axis=1).astype(BF)
        kv = norm_matmul(memf, g_memkv[l][None, :], wkv, ones_kv, BF, tm=memf.shape[0], tn=2 * MEM_WIDTH)
        xf = cross_attention(xf.reshape(b, s, d), g_mem[l][None, :], w_mq[l].astype(BF),
                             kv.reshape(b, -1, 2 * MEM_WIDTH), w_mo[l].astype(BF), tm=256).reshape(t, d)
        xf = hier_moe(xf, g_ffn[l], w_group[l], b_group[l], w_router[l], b_router[l],
                      w1.reshape(-1, d, EXPERT_FF), w3.reshape(-1, d, EXPERT_FF), w2.reshape(-1, EXPERT_FF, d),
                      l, g_final, final_norm=(l == depth - 1))
    return xf.reshape(b, s, d)
```

```python
import functools

import numpy as np
import jax
import jax.numpy as jnp
from jax import lax
from jax.experimental import pallas as pl
from jax.experimental.pallas import tpu as pltpu

D_MODEL = 2048
HEAD_DIM = 128
A_HEADS = 8
B_HEADS = 4
POOL_GROUPS = 4
POOL_WINDOWS = (2, 4, 8, 16)
A_WIDTH = A_HEADS * HEAD_DIM
B_WIDTH = B_HEADS * HEAD_DIM
C_WIDTH = POOL_GROUPS * HEAD_DIM
MOBA_BLOCK = 256
MOBA_TOPK = 3
DIL_PAIRS = ((128, 1), (512, 4), (2048, 16))
DIL_OFFSETS = 128
DIL_UNIT = 2048
MEM_HEADS = 4
MEM_WIDTH = MEM_HEADS * HEAD_DIM
N_GROUPS = 4
EXPERTS_PER_GROUP = 8
N_EXPERTS = N_GROUPS * EXPERTS_PER_GROUP
EXPERT_FF = D_MODEL // 4
RMS_EPS = 1e-6
SCALE = HEAD_DIM ** -0.5
LOG2E = 1.4426950408889634

LANES = 128
MOBA_SLOTS = 32
EXPERT_TILE = 256
NEG = -(2.0 ** 100)
VMEM_LIMIT = 56 * 1024 * 1024

BF = jnp.bfloat16
F32 = jnp.float32
NT_DIMS = (((1,), (1,)), ((), ()))


def _params(*sem):
    return pltpu.CompilerParams(dimension_semantics=sem, vmem_limit_bytes=VMEM_LIMIT)


def _rms(x, g):
    return x * lax.rsqrt(jnp.mean(x * x, axis=-1, keepdims=True) + RMS_EPS) * g


def _split3(x):
    hi = x.astype(BF).astype(F32)
    r1 = x - hi
    mid = r1.astype(BF).astype(F32)
    lo = (r1 - mid).astype(BF).astype(F32)
    return hi, mid, lo


def _norm_matmul_kernel(x_ref, g_ref, w_ref, cs_ref, o_ref, hn_ref):
    @pl.when(pl.program_id(1) == 0)
    def _():
        hn_ref[...] = _rms(x_ref[...], g_ref[...]).astype(BF)

    acc = jnp.dot(hn_ref[...], w_ref[...], preferred_element_type=F32)
    o_ref[...] = (acc * cs_ref[...]).astype(o_ref.dtype)


def norm_matmul(x, g, w, colscale, out_dtype, tm, tn):
    m, d = x.shape
    n = w.shape[1]
    return pl.pallas_call(
        _norm_matmul_kernel,
        grid=(m // tm, n // tn),
        in_specs=[
            pl.BlockSpec((tm, d), lambda i, j: (i, 0)),
            pl.BlockSpec((1, d), lambda i, j: (0, 0)),
            pl.BlockSpec((d, tn), lambda i, j: (0, j)),
            pl.BlockSpec((1, tn), lambda i, j: (0, j)),
        ],
        out_specs=pl.BlockSpec((tm, tn), lambda i, j: (i, j)),
        out_shape=jax.ShapeDtypeStruct((m, n), out_dtype),
        scratch_shapes=[pltpu.VMEM((tm, d), BF)],
        compiler_params=_params("parallel", "arbitrary"),
        name="norm_matmul",
    )(x, g, w, colscale)


MOBA_STEP = 2 * MOBA_BLOCK


MOBA_HEADS_PER_STEP = 4


def _moba_setup_head(k, v_ref, kaug_ref, kfull_ref, vt_ref, kmean_ref, cols, nblk):
    kfull_ref[:, :HEAD_DIM] = k
    kfull_ref[:, HEAD_DIM:] = kaug_ref[...]
    km = jnp.mean(k.astype(F32).reshape(nblk, MOBA_BLOCK, HEAD_DIM), axis=1)
    if nblk < MOBA_SLOTS:
        km = jnp.concatenate([km, jnp.zeros((MOBA_SLOTS - nblk, HEAD_DIM), F32)], axis=0)
    pad = jnp.zeros((LANES - 3 * MOBA_SLOTS, HEAD_DIM), F32)
    kmean_ref[...] = jnp.concatenate([km, km, km, pad], axis=0).astype(BF)

    def transpose_v(jj, carry):
        start = pl.multiple_of(jj * MOBA_STEP, MOBA_STEP)
        vt_ref[jj] = v_ref[pl.ds(start, MOBA_STEP), cols].astype(F32).T.astype(BF)
        return carry

    lax.fori_loop(0, nblk // 2, transpose_v, 0)


def _moba_query_features(qts, kmeans, slopes, i):
    n = len(qts)
    bq = qts[0].shape[1]
    feat = lax.broadcasted_iota(jnp.int32, (LANES, bq), 0)
    kblk = feat & (MOBA_SLOTS - 1)
    is_feat = feat < 3 * MOBA_SLOTS
    past = jnp.logical_and(kblk < i, is_feat)
    own = jnp.logical_and(kblk == i, is_feat)
    featf = feat.astype(F32)
    kblkf = kblk.astype(F32)
    blkdist = (i - kblk).astype(F32) * float(MOBA_BLOCK)
    gates = [jnp.dot(kmeans[h], qts[h], preferred_element_type=F32) for h in range(n)]
    g = [jnp.where(past, gates[h], -jnp.inf) for h in range(n)]
    sel = [jnp.zeros((LANES, bq), jnp.bool_) for _ in range(n)]
    for _ in range(MOBA_TOPK):
        for h in range(n):
            mx = jnp.max(g[h], axis=0, keepdims=True)
            first = jnp.min(jnp.where(g[h] == mx, featf, 1e9), axis=0, keepdims=True)
            pick = kblkf == first
            sel[h] = jnp.logical_or(sel[h], pick)
            g[h] = jnp.where(pick, -jnp.inf, g[h])
    out = []
    for h in range(n):
        bias = jnp.where(jnp.logical_and(sel[h], past), -slopes[h] * blkdist, jnp.where(own, 0.0, NEG))
        b_hi, b_mid, b_lo = _split3(bias)
        s_hi, s_mid, s_lo = _split3(jnp.full((LANES, bq), slopes[h], F32))
        qaug = jnp.where(feat < MOBA_SLOTS, b_hi,
               jnp.where(feat < 2 * MOBA_SLOTS, b_mid,
               jnp.where(feat < 3 * MOBA_SLOTS, b_lo,
               jnp.where(feat == 3 * MOBA_SLOTS, s_hi,
               jnp.where(feat == 3 * MOBA_SLOTS + 1, s_mid,
               jnp.where(feat == 3 * MOBA_SLOTS + 2, s_lo, 0.0))))))
        out.append(jnp.concatenate([qts[h], qaug.astype(BF)], axis=0))
    return out


def _moba_kernel(slopes_ref, q_ref, k_ref, v_ref, kaug_ref, o_ref, kfull_ref, vt_ref, kmean_ref,
                 m_ref, l_ref, alpha_ref, acc_ref, p_ref, *, nblk):
    hg = pl.program_id(1)
    i = pl.program_id(2)
    bq = MOBA_BLOCK
    heads = range(MOBA_HEADS_PER_STEP)
    cols = [slice(g * HEAD_DIM, (g + 1) * HEAD_DIM) for g in heads]

    @pl.when(i == 0)
    def _():
        for g in heads:
            _moba_setup_head(k_ref[:, cols[g]], v_ref, kaug_ref, kfull_ref.at[g], vt_ref.at[g],
                             kmean_ref.at[g], cols[g], nblk)

    q_full = _moba_query_features(
        [q_ref[:, cols[g]].astype(F32).T.astype(BF) for g in heads],
        [kmean_ref[g] for g in heads],
        [slopes_ref[hg * MOBA_HEADS_PER_STEP + g] * LOG2E for g in heads], i)

    for g in heads:
        m_ref[g] = jnp.full((1, bq), -jnp.inf, F32)
        l_ref[g] = jnp.zeros((1, bq), F32)
        acc_ref[g] = jnp.zeros((HEAD_DIM, bq), F32)

    def scores(g, jj, causal):
        start = pl.multiple_of(jj * MOBA_STEP, MOBA_STEP)
        kj = kfull_ref[g, pl.ds(start, MOBA_STEP), :]
        s = jnp.dot(kj, q_full[g], preferred_element_type=F32)
        if causal:
            kpos = start + lax.broadcasted_iota(jnp.int32, (MOBA_STEP, bq), 0)
            qpos = i * bq + lax.broadcasted_iota(jnp.int32, (MOBA_STEP, bq), 1)
            s = jnp.where(qpos >= kpos, s, NEG)
        return s

    def softmax(g, s):
        m = m_ref[g]
        m_new = jnp.maximum(m, jnp.max(s, axis=0, keepdims=True))
        alpha = jnp.exp2(m - m_new)
        p = jnp.exp2(s - m_new)
        l_ref[g] = alpha * l_ref[g] + jnp.sum(p, axis=0, keepdims=True)
        m_ref[g] = m_new
        alpha_ref[g] = alpha
        p_ref[g] = p.astype(BF)

    def accumulate(g, jj):
        acc_ref[g] = alpha_ref[g] * acc_ref[g] + jnp.dot(vt_ref[g, jj], p_ref[g],
                                                         preferred_element_type=F32)

    def pipelined(jj, causal):
        s = [scores(g, jj, causal) for g in heads]
        for g in heads:
            accumulate(g, jj - 1)
        for g in heads:
            softmax(g, s[g])

    last = i // 2
    s0 = [scores(g, 0, True) for g in heads]
    for g in heads:
        softmax(g, s0[g])

    def body(jj, carry):
        pipelined(jj, False)
        return carry

    lax.fori_loop(1, last, body, 0)

    @pl.when(last > 0)
    def _():
        pipelined(last, True)

    for g in heads:
        accumulate(g, last)
        o_ref[:, cols[g]] = (acc_ref[g] / l_ref[g]).T.astype(o_ref.dtype)


def moba_attention(pa, kaug, slopes):
    b, s, _ = pa.shape
    nblk = s // MOBA_BLOCK
    hps = MOBA_HEADS_PER_STEP
    groups = A_HEADS // hps
    assert s % MOBA_STEP == 0 and nblk <= MOBA_SLOTS
    return pl.pallas_call(
        functools.partial(_moba_kernel, nblk=nblk),
        grid=(b, groups, nblk),
        in_specs=[
            pl.BlockSpec(memory_space=pltpu.SMEM),
            pl.BlockSpec((None, MOBA_BLOCK, hps * HEAD_DIM), lambda bi, h, i: (bi, i, h)),
            pl.BlockSpec((None, s, hps * HEAD_DIM), lambda bi, h, i: (bi, 0, groups + h),
                         pipeline_mode=pl.Buffered(1)),
            pl.BlockSpec((None, s, hps * HEAD_DIM), lambda bi, h, i: (bi, 0, 2 * groups + h),
                         pipeline_mode=pl.Buffered(1)),
            pl.BlockSpec((s, LANES), lambda bi, h, i: (0, 0), pipeline_mode=pl.Buffered(1)),
        ],
        out_specs=pl.BlockSpec((None, MOBA_BLOCK, hps * HEAD_DIM), lambda bi, h, i: (bi, i, h)),
        out_shape=jax.ShapeDtypeStruct((b, s, A_WIDTH), BF),
        scratch_shapes=[
            pltpu.VMEM((hps, s, 2 * HEAD_DIM), BF),
            pltpu.VMEM((hps, nblk // 2, HEAD_DIM, MOBA_STEP), BF),
            pltpu.VMEM((hps, LANES, HEAD_DIM), BF),
            pltpu.VMEM((hps, 1, MOBA_BLOCK), F32),
            pltpu.VMEM((hps, 1, MOBA_BLOCK), F32),
            pltpu.VMEM((hps, 1, MOBA_BLOCK), F32),
            pltpu.VMEM((hps, HEAD_DIM, MOBA_BLOCK), F32),
            pltpu.VMEM((hps, MOBA_STEP, MOBA_BLOCK), BF),
        ],
        compiler_params=_params("parallel", "parallel", "arbitrary"),
        name="moba_attention",
    )(slopes, pa, pa, pa, kaug)


def moba_key_features(s):
    t = np.arange(s)
    f = np.arange(LANES)
    onehot = ((f[None, :] % MOBA_SLOTS) == (t[:, None] // MOBA_BLOCK)) & (f[None, :] < 3 * MOBA_SLOTS)
    ramp = ((f[None, :] >= 3 * MOBA_SLOTS) & (f[None, :] < 3 * MOBA_SLOTS + 3)) * (t[:, None] % MOBA_BLOCK)
    return jnp.asarray(onehot.astype(np.float32) + ramp.astype(np.float32), BF)


def _dil_kernel(slopes_ref, q_ref, kc_ref, kp_ref, vc_ref, vp_ref, o_ref, acc_ref, m_ref, l_ref):
    h = pl.program_id(1)
    u = pl.program_id(2)
    slope = slopes_ref[h] * LOG2E
    n = DIL_OFFSETS

    qi = lax.broadcasted_iota(jnp.int32, (n, 2 * n), 0)
    ki = lax.broadcasted_iota(jnp.int32, (n, 2 * n), 1)
    steps = qi + n - ki
    valid = jnp.logical_and(steps >= 0, steps <= n)
    stepsf = steps.astype(F32)
    first_ok = jnp.logical_or(ki >= n, u > 0)

    def keys(cur_ref, prev_ref, start, blk, dil):
        if blk > 0:
            return cur_ref[pl.ds(start - n * dil, 2 * n, stride=dil), :].astype(BF)
        before = prev_ref[pl.ds(DIL_UNIT - n * dil + start, n, stride=dil), :]
        return jnp.concatenate([before, cur_ref[pl.ds(start, n, stride=dil), :]], axis=0).astype(BF)

    for gi, (window, dil) in enumerate(DIL_PAIRS):
        nbk = DIL_UNIT // (dil * n)
        pen = jnp.where(valid, -slope * float(dil) * stepsf, NEG)
        blocks = [(blk * n * dil + r, blk) for r in range(dil) for blk in range(nbk)]
        rows = lambda start: pl.ds(start, n, stride=dil)
        scores = []
        for start, blk in blocks:
            q = q_ref[rows(start), :].astype(BF)
            s = lax.dot_general(q, keys(kc_ref, kp_ref, start, blk, dil), NT_DIMS,
                                preferred_element_type=F32) + pen
            scores.append(s if blk > 0 else jnp.where(first_ok, s, NEG))
        probs = []
        for (start, blk), s in zip(blocks, scores):
            m = jnp.max(s, axis=1, keepdims=True)
            p = jnp.exp2(s - m)
            l = jnp.sum(p, axis=1, keepdims=True)
            m_ref.at[gi][rows(start), :] = jnp.broadcast_to(m, (n, HEAD_DIM))
            l_ref.at[gi][rows(start), :] = jnp.broadcast_to(l, (n, HEAD_DIM))
            probs.append(p.astype(BF))
        for (start, blk), p in zip(blocks, probs):
            acc_ref.at[gi][rows(start), :] = jnp.dot(p, keys(vc_ref, vp_ref, start, blk, dil),
                                                     preferred_element_type=F32)

    m_all = jnp.maximum(jnp.maximum(m_ref[0], m_ref[1]), m_ref[2])
    num = jnp.zeros((DIL_UNIT, HEAD_DIM), F32)
    den = jnp.zeros((DIL_UNIT, HEAD_DIM), F32)
    for gi in range(len(DIL_PAIRS)):
        w = jnp.exp2(m_ref[gi] - m_all)
        num = num + w * acc_ref[gi]
        den = den + w * l_ref[gi]
    o_ref[...] = (num / den).astype(o_ref.dtype)


def dilated_attention(pb, slopes):
    b, s, _ = pb.shape
    assert s % DIL_UNIT == 0
    cur = lambda off: (lambda bi, h, u: (bi, u, off + h))
    prev = lambda off: (lambda bi, h, u: (bi, jnp.maximum(u - 1, 0), off + h))
    blk = (None, DIL_UNIT, HEAD_DIM)
    return pl.pallas_call(
        _dil_kernel,
        grid=(b, B_HEADS, s // DIL_UNIT),
        in_specs=[
            pl.BlockSpec(memory_space=pltpu.SMEM),
            pl.BlockSpec(blk, cur(0)),
            pl.BlockSpec(blk, cur(B_HEADS)),
            pl.BlockSpec(blk, prev(B_HEADS)),
            pl.BlockSpec(blk, cur(2 * B_HEADS)),
            pl.BlockSpec(blk, prev(2 * B_HEADS)),
        ],
        out_specs=pl.BlockSpec(blk, cur(0)),
        out_shape=jax.ShapeDtypeStruct((b, s, B_WIDTH), BF),
        scratch_shapes=[
            pltpu.VMEM((3, DIL_UNIT, HEAD_DIM), F32),
            pltpu.VMEM((3, DIL_UNIT, HEAD_DIM), F32),
            pltpu.VMEM((3, DIL_UNIT, HEAD_DIM), F32),
        ],
        compiler_params=_params("parallel", "parallel", "arbitrary"),
        name="dilated_attention",
    )(slopes, pb, pb, pb, pb, pb)


POOL_HALO = 16


def _pool_kernel(u_ref, halo_ref, w_ref, sc_ref, o_ref, ext_ref, *, tm):
    i = pl.program_id(1)
    halo = halo_ref[...]
    ext_ref[:POOL_HALO, :] = jnp.where(i > 0, halo, jnp.zeros_like(halo))
    ext_ref[POOL_HALO:, :] = u_ref[...]
    t = i * tm + lax.broadcasted_iota(jnp.int32, (tm, 1), 0)
    for gi, win in enumerate(POOL_WINDOWS):
        cols = slice(gi * HEAD_DIM, (gi + 1) * HEAD_DIM)
        tot = ext_ref[POOL_HALO:, cols]
        for back in range(1, win):
            tot = tot + ext_ref[pl.ds(POOL_HALO - back, tm), cols]
        cnt = jnp.minimum(t + 1, win).astype(F32)
        pooled = (tot / cnt - ext_ref[POOL_HALO:, cols]).astype(BF)
        y = jnp.dot(pooled, w_ref[gi], preferred_element_type=F32)
        o_ref[:, cols] = (y * sc_ref[:, cols]).astype(o_ref.dtype)


def pool_mixer(pb, pool_w, pool_scale, tm):
    b, s, _ = pb.shape
    ucol = 3 * B_WIDTH // C_WIDTH
    return pl.pallas_call(
        functools.partial(_pool_kernel, tm=tm),
        grid=(b, s // tm),
        in_specs=[
            pl.BlockSpec((None, tm, C_WIDTH), lambda bi, i: (bi, i, ucol)),
            pl.BlockSpec((None, POOL_HALO, C_WIDTH),
                         lambda bi, i: (bi, jnp.maximum(i * (tm // POOL_HALO) - 1, 0), ucol)),
            pl.BlockSpec((POOL_GROUPS, HEAD_DIM, HEAD_DIM), lambda bi, i: (0, 0, 0)),
            pl.BlockSpec((1, C_WIDTH), lambda bi, i: (0, 0)),
        ],
        out_specs=pl.BlockSpec((None, tm, C_WIDTH), lambda bi, i: (bi, i, 0)),
        out_shape=jax.ShapeDtypeStruct((b, s, C_WIDTH), BF),
        scratch_shapes=[pltpu.VMEM((tm + POOL_HALO, C_WIDTH), F32)],
        compiler_params=_params("parallel", "parallel"),
        name="pool_mixer",
    )(pb, pb, pool_w, pool_scale)


def _out_proj_kernel(x_ref, oa_ref, ob_ref, oc_ref, w_ref, o_ref):
    acc = jnp.dot(oa_ref[...], w_ref[:A_WIDTH, :], preferred_element_type=F32)
    acc += jnp.dot(ob_ref[...], w_ref[A_WIDTH:A_WIDTH + B_WIDTH, :], preferred_element_type=F32)
    acc += jnp.dot(oc_ref[...], w_ref[A_WIDTH + B_WIDTH:, :], preferred_element_type=F32)
    o_ref[...] = x_ref[...] + acc


def out_proj(x, oa, ob, oc, w, tm):
    t, d = x.shape
    return pl.pallas_call(
        _out_proj_kernel,
        grid=(t // tm,),
        in_specs=[
            pl.BlockSpec((tm, d), lambda i: (i, 0)),
            pl.BlockSpec((tm, A_WIDTH), lambda i: (i, 0)),
            pl.BlockSpec((tm, B_WIDTH), lambda i: (i, 0)),
            pl.BlockSpec((tm, C_WIDTH), lambda i: (i, 0)),
            pl.BlockSpec(w.shape, lambda i: (0, 0)),
        ],
        out_specs=pl.BlockSpec((tm, d), lambda i: (i, 0)),
        out_shape=jax.ShapeDtypeStruct((t, d), F32),
        compiler_params=_params("parallel"),
        name="out_proj",
    )(x, oa, ob, oc, w)


def _cross_kernel(x_ref, g_ref, wq_ref, kv_ref, wo_ref, o_ref):
    x = x_ref[...]
    hn = _rms(x, g_ref[...]).astype(BF)
    q = (jnp.dot(hn, wq_ref[...], preferred_element_type=F32) * SCALE).astype(BF)
    outs = []
    for hd in range(MEM_HEADS):
        cols = slice(hd * HEAD_DIM, (hd + 1) * HEAD_DIM)
        k = kv_ref[:, cols]
        v = kv_ref[:, MEM_WIDTH + hd * HEAD_DIM:MEM_WIDTH + (hd + 1) * HEAD_DIM]
        s = lax.dot_general(q[:, cols], k, NT_DIMS, preferred_element_type=F32)
        m = jnp.max(s, axis=1, keepdims=True)
        p = jnp.exp(s - m)
        l = jnp.sum(p, axis=1, keepdims=True)
        outs.append((jnp.dot(p.astype(BF), v, preferred_element_type=F32) / l).astype(BF))
    o = jnp.concatenate(outs, axis=1)
    o_ref[...] = x + jnp.dot(o, wo_ref[...], preferred_element_type=F32)


def cross_attention(x3, g, wq, kv, wo, tm):
    b, s, d = x3.shape
    mem_len = kv.shape[1]
    return pl.pallas_call(
        _cross_kernel,
        grid=(b, s // tm),
        in_specs=[
            pl.BlockSpec((None, tm, d), lambda bi, i: (bi, i, 0)),
            pl.BlockSpec((1, d), lambda bi, i: (0, 0)),
            pl.BlockSpec(wq.shape, lambda bi, i: (0, 0)),
            pl.BlockSpec((None, mem_len, 2 * MEM_WIDTH), lambda bi, i: (bi, 0, 0)),
            pl.BlockSpec(wo.shape, lambda bi, i: (0, 0)),
        ],
        out_specs=pl.BlockSpec((None, tm, d), lambda bi, i: (bi, i, 0)),
        out_shape=jax.ShapeDtypeStruct((b, s, d), F32),
        compiler_params=_params("parallel", "parallel"),
        name="cross_attention",
    )(x3, g, wq, kv, wo)


def _router_kernel(x_ref, g_ref, whi_ref, wlo_ref, b_ref, hn_ref, meta_ref, cnt_ref, base_ref, *, tm):
    @pl.when(pl.program_id(0) == 0)
    def _():
        base_ref[...] = jnp.zeros_like(base_ref)

    hn = _rms(x_ref[...], g_ref[...])
    hn_ref[...] = hn
    hi = hn.astype(BF)
    lo = (hn - hi.astype(F32)).astype(BF)
    logits = (jnp.dot(hi, whi_ref[...], preferred_element_type=F32)
              + jnp.dot(lo, whi_ref[...], preferred_element_type=F32)
              + jnp.dot(hi, wlo_ref[...], preferred_element_type=F32)) + b_ref[...]
    lane = lax.broadcasted_iota(jnp.int32, (tm, LANES), 1)
    lanef = lane.astype(F32)

    def first_argmax(vals):
        mx = jnp.max(vals, axis=1, keepdims=True)
        idx = jnp.min(jnp.where(vals == mx, lanef, 1e9), axis=1, keepdims=True)
        return mx, idx

    is_group = lane < N_GROUPS
    gmax, gsel = first_argmax(jnp.where(is_group, logits, -jnp.inf))
    gsum = jnp.sum(jnp.where(is_group, jnp.exp(logits - gmax), 0.0), axis=1, keepdims=True)
    gw = 1.0 / gsum
    lane_group = ((lane - N_GROUPS) // EXPERTS_PER_GROUP).astype(F32)
    in_group = jnp.logical_and(jnp.logical_and(lane >= N_GROUPS, lane < N_GROUPS + N_EXPERTS),
                               lane_group == gsel)
    el = jnp.where(in_group, logits, -jnp.inf)
    v1, i1 = first_argmax(el)
    v2, i2 = first_argmax(jnp.where(lanef == i1, -jnp.inf, el))
    t = jnp.exp(v2 - v1)
    w1 = gw / (1.0 + t)
    w2 = gw * t / (1.0 + t)
    e1 = i1 - N_GROUPS
    e2 = i2 - N_GROUPS
    oh1 = lanef == e1
    oh2 = lanef == e2
    cnt = jnp.where(jnp.logical_or(oh1, oh2), 1.0, 0.0)
    r = lax.broadcasted_iota(jnp.int32, (tm, tm), 0)
    c = lax.broadcasted_iota(jnp.int32, (tm, tm), 1)
    lower = jnp.where(r > c, 1.0, 0.0).astype(BF)
    before = jnp.dot(lower, cnt.astype(BF), preferred_element_type=F32) + base_ref[...]
    rank1 = jnp.sum(jnp.where(oh1, before, 0.0), axis=1, keepdims=True)
    rank2 = jnp.sum(jnp.where(oh2, before, 0.0), axis=1, keepdims=True)
    base_ref[...] += jnp.sum(cnt, axis=0, keepdims=True)
    cnt_ref[...] = base_ref[...]
    meta_ref[...] = jnp.where(lane == 0, e1, jnp.where(lane == 1, e2, jnp.where(lane == 2, rank1,
                    jnp.where(lane == 3, rank2, jnp.where(lane == 4, w1, jnp.where(lane == 5, w2, 0.0))))))


def router(x, g, whi, wlo, bias, tm):
    t, d = x.shape
    return pl.pallas_call(
        functools.partial(_router_kernel, tm=tm),
        grid=(t // tm,),
        in_specs=[
            pl.BlockSpec((tm, d), lambda i: (i, 0)),
            pl.BlockSpec((1, d), lambda i: (0, 0)),
            pl.BlockSpec((d, LANES), lambda i: (0, 0)),
            pl.BlockSpec((d, LANES), lambda i: (0, 0)),
            pl.BlockSpec((1, LANES), lambda i: (0, 0)),
        ],
        out_specs=[
            pl.BlockSpec((tm, d), lambda i: (i, 0)),
            pl.BlockSpec((tm, LANES), lambda i: (i, 0)),
            pl.BlockSpec((1, LANES), lambda i: (0, 0)),
        ],
        out_shape=[
            jax.ShapeDtypeStruct((t, d), F32),
            jax.ShapeDtypeStruct((t, LANES), F32),
            jax.ShapeDtypeStruct((1, LANES), F32),
        ],
        scratch_shapes=[pltpu.VMEM((1, LANES), F32)],
        compiler_params=_params("arbitrary"),
        name="moe_router",
    )(x, g, whi, wlo, bias)


def _positions_kernel(meta_ref, offs_ref, o_ref, *, tm):
    meta = meta_ref[...]
    lane = lax.broadcasted_iota(jnp.int32, (tm, LANES), 1)
    lanef = lane.astype(F32)
    offs = offs_ref[...]
    pos1 = jnp.sum(jnp.where(lanef == meta[:, 0:1], offs, 0.0), axis=1, keepdims=True) + meta[:, 2:3]
    pos2 = jnp.sum(jnp.where(lanef == meta[:, 1:2], offs, 0.0), axis=1, keepdims=True) + meta[:, 3:4]
    o_ref[...] = jnp.where(lane == 0, pos1, jnp.where(lane == 1, pos2, 0.0))


def positions(meta, offs, tm):
    t = meta.shape[0]
    return pl.pallas_call(
        functools.partial(_positions_kernel, tm=tm),
        grid=(t // tm,),
        in_specs=[pl.BlockSpec((tm, LANES), lambda i: (i, 0)), pl.BlockSpec((1, LANES), lambda i: (0, 0))],
        out_specs=pl.BlockSpec((tm, LANES), lambda i: (i, 0)),
        out_shape=jax.ShapeDtypeStruct((t, LANES), F32),
        compiler_params=_params("parallel"),
        name="moe_positions",
    )(meta, offs)


DMA_ISSUE_UNROLL = 8


def _row_copy(src, dst, sem):
    return pltpu.make_async_copy(src, dst, sem)


def _dispatch_kernel(pos_ref, hn_ref, xs_in_ref, xs_ref, sem, *, tm):
    del xs_in_ref
    base = pl.program_id(0) * tm

    def issue(r, carry):
        for k in range(2):
            p = pos_ref[2 * (base + r) + k]
            _row_copy(hn_ref.at[pl.ds(r, 1), :], xs_ref.at[pl.ds(p, 1), :], sem).start()
        return carry

    lax.fori_loop(0, tm, issue, 0, unroll=DMA_ISSUE_UNROLL)

    def drain(r, carry):
        for k in range(2):
            _row_copy(hn_ref.at[pl.ds(0, 1), :], xs_ref.at[pl.ds(0, 1), :], sem).wait()
        return carry

    lax.fori_loop(0, tm, drain, 0, unroll=True)


def dispatch(pos, hn, n_rows, tm):
    t, d = hn.shape
    xs0 = jnp.zeros((n_rows, d), hn.dtype)
    return pl.pallas_call(
        functools.partial(_dispatch_kernel, tm=tm),
        grid_spec=pltpu.PrefetchScalarGridSpec(
            num_scalar_prefetch=1,
            grid=(t // tm,),
            in_specs=[pl.BlockSpec((tm, d), lambda i, pos: (i, 0)), pl.BlockSpec(memory_space=pl.ANY)],
            out_specs=pl.BlockSpec(memory_space=pl.ANY),
            scratch_shapes=[pltpu.SemaphoreType.DMA(())],
        ),
        out_shape=jax.ShapeDtypeStruct((n_rows, d), hn.dtype),
        input_output_aliases={2: 0},
        compiler_params=_params("arbitrary"),
        name="moe_dispatch",
    )(pos, hn, xs0)


def _expert_kernel(te_ref, nu_ref, xs_ref, w1_ref, w3_ref, w2_ref, o_ref, b1_ref, b3_ref, b2_ref):
    i = pl.program_id(0)
    new_expert = jnp.logical_or(i == 0, te_ref[i] != te_ref[jnp.maximum(i - 1, 0)])

    @pl.when(new_expert)
    def _():
        b1_ref[...] = w1_ref[...].astype(BF)
        b3_ref[...] = w3_ref[...].astype(BF)
        b2_ref[...] = w2_ref[...].astype(BF)

    @pl.when(i < nu_ref[0])
    def _():
        x = xs_ref[...].astype(BF)
        a = jnp.dot(x, b1_ref[...], preferred_element_type=F32)
        c = jnp.dot(x, b3_ref[...], preferred_element_type=F32)
        hid = (a * (1.0 / (1.0 + jnp.exp(-a))) * c).astype(BF)
        o_ref[...] = jnp.dot(hid, b2_ref[...], preferred_element_type=F32)

    @pl.when(i >= nu_ref[0])
    def _():
        o_ref[...] = jnp.zeros_like(o_ref)


def expert_mlp(tile_expert, n_used, xs, w1, w3, w2):
    n_rows, d = xs.shape
    f = w1.shape[2]
    return pl.pallas_call(
        _expert_kernel,
        grid_spec=pltpu.PrefetchScalarGridSpec(
            num_scalar_prefetch=2,
            grid=(n_rows // EXPERT_TILE,),
            in_specs=[
                pl.BlockSpec((EXPERT_TILE, d), lambda i, te, nu: (i, 0)),
                pl.BlockSpec((None, d, f), lambda i, te, nu: (te[i], 0, 0)),
                pl.BlockSpec((None, d, f), lambda i, te, nu: (te[i], 0, 0)),
                pl.BlockSpec((None, f, d), lambda i, te, nu: (te[i], 0, 0)),
            ],
            out_specs=pl.BlockSpec((EXPERT_TILE, d), lambda i, te, nu: (i, 0)),
            scratch_shapes=[pltpu.VMEM((d, f), BF), pltpu.VMEM((d, f), BF), pltpu.VMEM((f, d), BF)],
        ),
        out_shape=jax.ShapeDtypeStruct((n_rows, d), F32),
        compiler_params=_params("arbitrary"),
        name="moe_experts",
    )(tile_expert, n_used, xs, w1, w3, w2)


def _combine_kernel(pos_ref, x_ref, meta_ref, gf_ref, ys_ref, o_ref, buf_ref, sem, *, tm, final_norm):
    base = pl.program_id(0) * tm

    def issue(r, carry):
        for k in range(2):
            p = pos_ref[2 * (base + r) + k]
            _row_copy(ys_ref.at[pl.ds(p, 1), :], buf_ref.at[k, pl.ds(r, 1), :], sem).start()
        return carry

    lax.fori_loop(0, tm, issue, 0, unroll=DMA_ISSUE_UNROLL)

    def drain(r, carry):
        for k in range(2):
            _row_copy(ys_ref.at[pl.ds(0, 1), :], buf_ref.at[k, pl.ds(0, 1), :], sem).wait()
        return carry

    lax.fori_loop(0, tm, drain, 0, unroll=True)
    meta = meta_ref[...]
    y = x_ref[...] + meta[:, 4:5] * buf_ref[0] + meta[:, 5:6] * buf_ref[1]
    if final_norm:
        y = _rms(y, gf_ref[...])
    o_ref[...] = y


def combine(pos, x, meta, g_final, ys, tm, final_norm):
    t, d = x.shape
    return pl.pallas_call(
        functools.partial(_combine_kernel, tm=tm, final_norm=final_norm),
        grid_spec=pltpu.PrefetchScalarGridSpec(
            num_scalar_prefetch=1,
            grid=(t // tm,),
            in_specs=[
                pl.BlockSpec((tm, d), lambda i, pos: (i, 0)),
                pl.BlockSpec((tm, LANES), lambda i, pos: (i, 0)),
                pl.BlockSpec((1, d), lambda i, pos: (0, 0)),
                pl.BlockSpec(memory_space=pl.ANY),
            ],
            out_specs=pl.BlockSpec((tm, d), lambda i, pos: (i, 0)),
            scratch_shapes=[pltpu.VMEM((2, tm, d), F32), pltpu.SemaphoreType.DMA(())],
        ),
        out_shape=jax.ShapeDtypeStruct((t, d), F32),
        compiler_params=_params("arbitrary"),
        name="moe_combine",
    )(pos, x, meta, g_final, ys)


def _alibi_slopes():
    n = A_HEADS + B_HEADS
    s = 2.0 ** (-8.0 * np.arange(1, n + 1) / n)
    is_b = (np.arange(n) % 3) == 2
    return jnp.asarray(s[~is_b], F32), jnp.asarray(s[is_b], F32)


def hier_moe(x, g_ffn, w_group, b_group, w_router, b_router, w1, w3, w2, layer, g_final, final_norm):
    t, d = x.shape
    wr = jnp.concatenate([w_group, w_router.transpose(1, 0, 2).reshape(d, N_EXPERTS),
                          jnp.zeros((d, LANES - N_GROUPS - N_EXPERTS), F32)], axis=1)
    br = jnp.concatenate([b_group, b_router.reshape(N_EXPERTS),
                          jnp.zeros((LANES - N_GROUPS - N_EXPERTS,), F32)])[None, :]
    whi = wr.astype(BF)
    wlo = (wr - whi.astype(F32)).astype(BF)
    hn, meta, counts = router(x, g_ffn[None, :], whi, wlo, br, tm=512)

    cnt = counts[0, :N_EXPERTS].astype(jnp.int32)
    ntile = (cnt + EXPERT_TILE - 1) // EXPERT_TILE
    tend = jnp.cumsum(ntile)
    tstart = tend - ntile
    n_used = tend[-1]
    n_tiles = (2 * t) // EXPERT_TILE + N_EXPERTS
    tile_ids = jnp.minimum(jnp.arange(n_tiles, dtype=jnp.int32), n_used - 1)
    tile_expert = jnp.sum(tile_ids[:, None] >= tend[None, :], axis=1).astype(jnp.int32)
    offs = jnp.zeros((1, LANES), F32).at[0, :N_EXPERTS].set((tstart * EXPERT_TILE).astype(F32))

    pos = positions(meta, offs, tm=512)[:, :2].astype(jnp.int32).reshape(-1)
    xs = dispatch(pos, hn, n_tiles * EXPERT_TILE, tm=256)
    ys = expert_mlp(tile_expert + layer * N_EXPERTS, n_used.reshape(1).astype(jnp.int32), xs, w1, w3, w2)
    return combine(pos, x, meta, g_final[None, :], ys, tm=256, final_norm=final_norm)


def kernel(x, mem, g_mix, w_in, w_out, pool_w, pool_scale, g_mem, g_memkv, w_mq, w_mk, w_mv, w_mo,
           g_ffn, w_group, b_group, w_router, b_router, w1, w3, w2, g_final):
    b, s, d = x.shape
    t = b * s
    depth = w_in.shape[0]
    slopes_a, slopes_b = _alibi_slopes()
    kaug = moba_key_features(s)
    n_a = 3 * A_WIDTH
    n_b = 3 * B_WIDTH + C_WIDTH
    cs_a = jnp.concatenate([jnp.full((A_WIDTH,), SCALE * LOG2E, F32), jnp.ones((2 * A_WIDTH,), F32)])[None, :]
    cs_b = jnp.concatenate([jnp.full((B_WIDTH,), SCALE * LOG2E, F32), jnp.ones((n_b - B_WIDTH,), F32)])[None, :]
    ones_kv = jnp.ones((1, 2 * MEM_WIDTH), F32)
    xf = x.reshape(t, d)
    memf = mem.reshape(-1, d)
    for l in range(depth):
        w_in_l = w_in[l].astype(BF)
        gl = g_mix[l][None, :]
        pa = norm_matmul(xf, gl, w_in_l[:, :n_a], cs_a, BF, tm=1024, tn=1024).reshape(b, s, n_a)
        pb = norm_matmul(xf, gl, w_in_l[:, n_a:], cs_b, F32, tm=1024, tn=1024).reshape(b, s, n_b)
        oa = moba_attention(pa, kaug, slopes_a)
        ob = dilated_attention(pb, slopes_b)
        oc = pool_mixer(pb, pool_w[l].astype(BF), pool_scale[l][None, :], tm=512)
        xf = out_proj(xf, oa.reshape(t, A_WIDTH), ob.reshape(t, B_WIDTH), oc.reshape(t, C_WIDTH),
                      w_out[l].astype(BF), tm=256)
        wkv = jnp.concatenate([w_mk[l], w_mv[l]], axis=1).astype(BF)
        kv = norm_matmul(memf, g_memkv[l][None, :], wkv, ones_kv, BF, tm=memf.shape[0], tn=2 * MEM_WIDTH)
        xf = cross_attention(xf.reshape(b, s, d), g_mem[l][None, :], w_mq[l].astype(BF),
                             kv.reshape(b, -1, 2 * MEM_WIDTH), w_mo[l].astype(BF), tm=256).reshape(t, d)
        xf = hier_moe(xf, g_ffn[l], w_group[l], b_group[l], w_router[l], b_router[l],
                      w1.reshape(-1, d, EXPERT_FF), w3.reshape(-1, d, EXPERT_FF), w2.reshape(-1, EXPERT_FF, d),
                      l, g_final, final_norm=(l == depth - 1))
    return xf.reshape(b, s, d)
```
